```python
import math
import jax, jax.numpy as jnp
from jax import lax
import numpy as np

D_MODEL = 1024
BATCH = 32
SEQ = 2048
DEPTH = 2

GRID_W = 64
CTX_LEN = 256
BRANCH_WIDTH = D_MODEL // 2
MIX_WIDTH = 2 * BRANCH_WIDTH
LRU_HEAD_DIM = 64
LRU_HEADS = BRANCH_WIDTH // LRU_HEAD_DIM
LRU_CONV = 4
LRU_C = 8.0
SGU_CHUNK = 128
SGU_GROUPS = 4
SGU_GROUP_DIM = BRANCH_WIDTH // SGU_GROUPS
SCONV_K = 3
DIFF_V_DIM = 128
DIFF_HEADS = BRANCH_WIDTH // DIFF_V_DIM
DIFF_HEAD_DIM = DIFF_V_DIM // 2
Q_BLOCK = 128
ROPE_BASE = 10000.0
AB_IN = 5 * BRANCH_WIDTH
CD_IN = 8 * BRANCH_WIDTH
N_AB = (DEPTH + 1) // 2
N_CD = DEPTH // 2
DEEPNORM_ALPHA = (2 * DEPTH) ** 0.25
DEEPNORM_BETA = (8 * DEPTH) ** -0.25
LN_EPS = 1e-5

kernel_name = "hybrid_lru_sgu_conv_diffattn_trunk"


def layer_norm(x, g, b):
    xf = x.astype(jnp.float32)
    mu = jnp.mean(xf, axis=-1, keepdims=True)
    var = jnp.mean(jnp.square(xf - mu), axis=-1, keepdims=True)
    return ((xf - mu) * lax.rsqrt(var + LN_EPS)).astype(x.dtype) * g + b


def rms_norm(x, g):
    xf = x.astype(jnp.float32)
    return (xf * lax.rsqrt(jnp.mean(jnp.square(xf), axis=-1, keepdims=True) + LN_EPS)).astype(x.dtype) * g


def modulation(cond, w, b):
    m = jax.nn.silu(cond) @ w + b
    shift, scale, gate = jnp.split(m[:, None, :], 3, axis=-1)
    return shift, scale, gate


def depthwise_conv(x, w, b=None):
    k = w.shape[0]
    y = lax.conv_general_dilated(x, w[:, None, :].astype(x.dtype), (1,), [(k // 2, k - 1 - k // 2)],
                                 dimension_numbers=('NWC', 'WIO', 'NWC'), feature_group_count=x.shape[-1])
    if b is not None:
        y = y + b
    return y


def rglru_coeffs(xc, w_a, b_a, w_x, b_x, lam):
    bsz, n, _ = xc.shape
    xh = xc.reshape(bsz, n, LRU_HEADS, LRU_HEAD_DIM)
    gate_r = jax.nn.sigmoid(jnp.einsum('blhi,hij->blhj', xh, w_a).reshape(bsz, n, BRANCH_WIDTH) + b_a)
    gate_i = jax.nn.sigmoid(jnp.einsum('blhi,hij->blhj', xh, w_x).reshape(bsz, n, BRANCH_WIDTH) + b_x)
    log_a = (-LRU_C * gate_r.astype(jnp.float32)) * jax.nn.softplus(-lam.astype(jnp.float32))
    a = jnp.exp(log_a)
    bterm = jnp.sqrt(-jnp.expm1(2.0 * log_a)) * (gate_i * xc).astype(jnp.float32)
    return a, bterm


def _scan_combine(left, right):
    a_l, h_l = left
    a_r, h_r = right
    return a_l * a_r, a_r * h_l + h_r


def linear_scan(a, b, reverse, h0=None):
    if h0 is not None:
        edge = -1 if reverse else 0
        b = b.at[:, edge].add(a[:, edge] * h0)
    return lax.associative_scan(_scan_combine, (a, b), reverse=reverse, axis=1)[1]


def spatial_gating(u, v, ln_g, ln_b, w_s, b_s):
    bsz, n, _ = v.shape
    vh = layer_norm(v, ln_g, ln_b).reshape(bsz, n // SGU_CHUNK, SGU_CHUNK, SGU_GROUPS, SGU_GROUP_DIM)
    s = jnp.einsum('gpq,bnqgc->bnpgc', w_s, vh) + b_s.T[:, :, None]
    return u * s.reshape(bsz, n, BRANCH_WIDTH)


def axial_rope(n_tokens):
    rows = n_tokens // GRID_W
    row = jnp.repeat(jnp.arange(rows, dtype=jnp.float32), GRID_W)
    col = jnp.tile(jnp.arange(GRID_W, dtype=jnp.float32), rows)
    n_freq = DIFF_HEAD_DIM // 4
    inv = ROPE_BASE ** (-jnp.arange(n_freq, dtype=jnp.float32) / n_freq)
    ang_r = row[:, None] * inv
    ang_c = col[:, None] * inv
    ang = jnp.concatenate([ang_r, ang_r, ang_c, ang_c], axis=-1)
    return jnp.cos(ang), jnp.sin(ang)


def apply_rope(x, cos, sin):
    n_freq = DIFF_HEAD_DIM // 4
    xs = x.reshape(x.shape[:-1] + (2, 2, n_freq))
    rot = jnp.stack([-xs[..., 1, :], xs[..., 0, :]], axis=-2).reshape(x.shape)
    cos = cos[:, None, None, :].astype(x.dtype)
    sin = sin[:, None, None, :].astype(x.dtype)
    return x * cos + rot * sin


def diff_attention(q, k, v, lam):
    s = jnp.einsum('bqhmd,bkhmd->bhmqk', q, k).astype(jnp.float32) * (DIFF_HEAD_DIM ** -0.5)
    p = jax.nn.softmax(s, axis=-1)
    a = p[:, :, 0] - lam * p[:, :, 1]
    return jnp.einsum('bhqk,bkhe->bqhe', a.astype(v.dtype), v)


def blocked_diff_attention(q, k, v, lam):
    bsz, n, nh, two, dh = q.shape
    qb = jnp.moveaxis(q.reshape(bsz, n // Q_BLOCK, Q_BLOCK, nh, two, dh), 1, 0)
    ob = lax.map(lambda qq: diff_attention(qq, k, v, lam), qb)
    return jnp.moveaxis(ob, 0, 1).reshape(bsz, n, nh, DIFF_V_DIM)


def ab_mixer(h, hc, w_in, w_out, conv_w, conv_b, w_a, b_a, w_x, b_x, lam,
             sgu_ln_g, sgu_ln_b, sgu_w, sgu_b, need_ctx_out):
    W = BRANCH_WIDTH
    p = h @ w_in
    pc = hc @ (w_in if need_ctx_out else w_in[:, :W])
    xl = depthwise_conv(p[..., :W], conv_w, conv_b)
    xc = depthwise_conv(pc[..., :W], conv_w, conv_b)
    h_lat = 0.0
    h_ctx_dirs = []
    for d, reverse in enumerate((False, True)):
        ac, bc = rglru_coeffs(xc, w_a[d], b_a[d], w_x[d], b_x[d], lam[d])
        hcd = linear_scan(ac, bc, reverse)
        h_end = hcd[:, 0] if reverse else hcd[:, -1]
        al, bl = rglru_coeffs(xl, w_a[d], b_a[d], w_x[d], b_x[d], lam[d])
        h_lat = h_lat + linear_scan(al, bl, reverse, h_end)
        h_ctx_dirs.append(hcd)

    def branches(pp, h_rec):
        y_a = h_rec.astype(pp.dtype) * jax.nn.silu(pp[..., W:2 * W])
        u = jax.nn.gelu(pp[..., 2 * W:3 * W])
        v = jax.nn.gelu(pp[..., 3 * W:4 * W])
        y_b = spatial_gating(u, v, sgu_ln_g, sgu_ln_b, sgu_w, sgu_b) * jax.nn.silu(pp[..., 4 * W:5 * W])
        return jnp.concatenate([y_a, y_b], axis=-1) @ w_out

    y = branches(p, h_lat)
    yc = branches(pc, h_ctx_dirs[0] + h_ctx_dirs[1]) if need_ctx_out else None
    return y, yc


def cd_mixer(h, hc, w_in, w_out, sconv_w, lam_vec, subln_g, lam_init, need_ctx_out):
    W = BRANCH_WIDTH
    bsz, n, _ = h.shape
    p = h @ w_in
    pc = hc @ (w_in if need_ctx_out else w_in[:, 5 * W:7 * W])
    kvc = pc[..., 5 * W:7 * W] if need_ctx_out else pc
    lam_f = lam_vec.astype(jnp.float32)
    lam = (jnp.exp(jnp.sum(lam_f[0] * lam_f[1])) - jnp.exp(jnp.sum(lam_f[2] * lam_f[3])) + lam_init)

    def heads_qk(t):
        return t.reshape(t.shape[0], t.shape[1], DIFF_HEADS, 2, DIFF_HEAD_DIM)

    def heads_v(t):
        return t.reshape(t.shape[0], t.shape[1], DIFF_HEADS, DIFF_V_DIM)

    cos, sin = axial_rope(n)
    q = apply_rope(heads_qk(p[..., 4 * W:5 * W]), cos, sin)
    k = apply_rope(heads_qk(p[..., 5 * W:6 * W]), cos, sin)
    v = heads_v(p[..., 6 * W:7 * W])
    kc = heads_qk(kvc[..., :W])
    vc = heads_v(kvc[..., W:])
    k_all = jnp.concatenate([kc, k], axis=1)
    v_all = jnp.concatenate([vc, v], axis=1)
    o = blocked_diff_attention(q, k_all, v_all, lam)

    def finish(pp, oo):
        y_c = pp[..., W:2 * W] * depthwise_conv(pp[..., 2 * W:3 * W] * pp[..., :W], sconv_w)
        y_c = y_c * jax.nn.silu(pp[..., 3 * W:4 * W])
        oo = rms_norm(oo, subln_g) * (1.0 - lam_init)
        y_d = oo.reshape(oo.shape[0], oo.shape[1], W) * jax.nn.silu(pp[..., 7 * W:8 * W])
        return jnp.concatenate([y_c, y_d], axis=-1) @ w_out

    y = finish(p, o)
    yc = None
    if need_ctx_out:
        oc = diff_attention(heads_qk(pc[..., 4 * W:5 * W]), kc, vc, lam)
        yc = finish(pc, oc)
    return y, yc


def setup_inputs(seed: int = 0) -> dict:
    key = jax.random.key(seed)
    ks = jax.random.split(key, 26)
    f32 = jnp.float32
    W = BRANCH_WIDTH

    def nrm(k, shape, s):
        return jax.random.normal(k, shape, f32) * s

    u = jax.random.uniform(ks[16], (N_AB, 2, W), f32, 0.9, 0.999)
    s = u ** (1.0 / LRU_C)
    return {
        "x": nrm(ks[0], (BATCH, SEQ, D_MODEL), 1.0),
        "c": nrm(ks[1], (BATCH, D_MODEL), 1.0),
        "ctx": nrm(ks[2], (BATCH, CTX_LEN, D_MODEL), 1.0),
        "c_ctx": nrm(ks[3], (D_MODEL,), 1.0),
        "w_mod": nrm(ks[4], (DEPTH, D_MODEL, 3 * D_MODEL), D_MODEL ** -0.5),
        "b_mod": nrm(ks[5], (DEPTH, 3 * D_MODEL), 0.02),
        "ln_g": 1.0 + nrm(ks[6], (DEPTH, D_MODEL), 0.02),
        "ln_b": nrm(ks[7], (DEPTH, D_MODEL), 0.02),
        "ab_w_in": nrm(ks[8], (N_AB, D_MODEL, AB_IN), D_MODEL ** -0.5),
        "ab_w_out": nrm(ks[9], (N_AB, MIX_WIDTH, D_MODEL), DEEPNORM_BETA * MIX_WIDTH ** -0.5),
        "lru_conv_w": nrm(ks[10], (N_AB, LRU_CONV, W), LRU_CONV ** -0.5),
        "lru_conv_b": nrm(ks[11], (N_AB, W), 0.02),
        "lru_w_a": nrm(ks[12], (N_AB, 2, LRU_HEADS, LRU_HEAD_DIM, LRU_HEAD_DIM), LRU_HEAD_DIM ** -0.5),
        "lru_b_a": nrm(ks[13], (N_AB, 2, W), 0.02),
        "lru_w_x": nrm(ks[14], (N_AB, 2, LRU_HEADS, LRU_HEAD_DIM, LRU_HEAD_DIM), LRU_HEAD_DIM ** -0.5),
        "lru_b_x": nrm(ks[15], (N_AB, 2, W), 0.02),
        "lru_lambda": jnp.log(s) - jnp.log1p(-s),
        "sgu_ln_g": 1.0 + nrm(ks[17], (N_AB, W), 0.02),
        "sgu_ln_b": nrm(ks[18], (N_AB, W), 0.02),
        "sgu_w": nrm(ks[19], (N_AB, SGU_GROUPS, SGU_CHUNK, SGU_CHUNK), SGU_CHUNK ** -0.5),
        "sgu_b": 1.0 + nrm(ks[20], (N_AB, SGU_GROUPS, SGU_CHUNK), 0.02),
        "cd_w_in": nrm(ks[21], (N_CD, D_MODEL, CD_IN), D_MODEL ** -0.5),
        "cd_w_out": nrm(ks[22], (N_CD, MIX_WIDTH, D_MODEL), DEEPNORM_BETA * MIX_WIDTH ** -0.5),
        "sconv_w": nrm(ks[23], (N_CD, SCONV_K, W), SCONV_K ** -0.5),
        "diff_lambda": nrm(ks[24], (N_CD, 4, DIFF_HEAD_DIM), 0.1),
        "diff_subln_g": 1.0 + nrm(ks[25], (N_CD, DIFF_V_DIM), 0.02),
    }


def reference(x, c, ctx, c_ctx, w_mod, b_mod, ln_g, ln_b, ab_w_in, ab_w_out, lru_conv_w, lru_conv_b,
              lru_w_a, lru_b_a, lru_w_x, lru_b_x, lru_lambda, sgu_ln_g, sgu_ln_b, sgu_w, sgu_b,
              cd_w_in, cd_w_out, sconv_w, diff_lambda, diff_subln_g):
    for l in range(DEPTH):
        need_ctx_out = l < DEPTH - 1
        shift, scale, gate = modulation(c, w_mod[l], b_mod[l])
        shift_c, scale_c, gate_c = modulation(c_ctx[None], w_mod[l], b_mod[l])
        h = x * (1.0 + scale) + shift
        hc = ctx * (1.0 + scale_c) + shift_c
        i = l // 2
        if l % 2 == 0:
            y, yc = ab_mixer(h, hc, ab_w_in[i], ab_w_out[i], lru_conv_w[i], lru_conv_b[i],
                             lru_w_a[i], lru_b_a[i], lru_w_x[i], lru_b_x[i], lru_lambda[i],
                             sgu_ln_g[i], sgu_ln_b[i], sgu_w[i], sgu_b[i], need_ctx_out)
        else:
            lam_init = 0.8 - 0.6 * math.exp(-0.3 * l)
            y, yc = cd_mixer(h, hc, cd_w_in[i], cd_w_out[i], sconv_w[i], diff_lambda[i],
                             diff_subln_g[i], lam_init, need_ctx_out)
        x = layer_norm(DEEPNORM_ALPHA * x + gate * y, ln_g[l], ln_b[l])
        if need_ctx_out:
            ctx = layer_norm(DEEPNORM_ALPHA * ctx + gate_c * yc, ln_g[l], ln_b[l])
    return x
```

```python
import functools
import math

import jax
import jax.numpy as jnp
from jax import lax
from jax.experimental import pallas as pl
from jax.experimental.pallas import tpu as pltpu

D_MODEL = 1024
DEPTH = 2
GRID_W = 64
W = D_MODEL // 2
LRU_HEAD_DIM = 64
LRU_C = 8.0
LRU_PAIR = 2 * LRU_HEAD_DIM
N_PAIR = W // LRU_PAIR
SGU_CHUNK = 128
SGU_GROUPS = 4
DIFF_V_DIM = 128
DIFF_HEADS = W // DIFF_V_DIM
DIFF_HEAD_DIM = DIFF_V_DIM // 2
ROPE_BASE = 10000.0
DEEPNORM_ALPHA = (2 * DEPTH) ** 0.25
LN_EPS = 1e-5
LAM_INIT_1 = 0.8 - 0.6 * math.exp(-0.3 * 1)

SUBLANES = 8
LANES = 128
VMEM_LIMIT = 56 * 1024 * 1024

F32 = jnp.float32
BF16 = jnp.bfloat16


def _silu(x):
    return x * jax.nn.sigmoid(x)


def _dot(a, b):
    return jnp.dot(a, b, preferred_element_type=F32)


def _params(*sem):
    return pltpu.CompilerParams(dimension_semantics=sem, vmem_limit_bytes=VMEM_LIMIT)


def _bcast_spec(arr, nd_grid):
    zeros = (0,) * arr.ndim
    return pl.BlockSpec(arr.shape, lambda *_: zeros)


def _mod_spec(arr):
    if arr.shape[0] == 1:
        return pl.BlockSpec((None, 1, D_MODEL), lambda b, i: (0, 0, 0))
    return pl.BlockSpec((None, 1, D_MODEL), lambda b, i: (b, 0, 0))


def _mod_kernel(cond_ref, w_ref, b_ref, o_ref):
    s = _silu(cond_ref[...]).astype(BF16)
    o_ref[...] = _dot(s, w_ref[...].astype(BF16)) + b_ref[...]


def _modulation(cond, w_mod, b_mod):
    n = cond.shape[0]
    return pl.pallas_call(
        _mod_kernel,
        grid=(DEPTH, 3),
        in_specs=[
            pl.BlockSpec((n, D_MODEL), lambda l, j: (0, 0)),
            pl.BlockSpec((None, D_MODEL, D_MODEL), lambda l, j: (l, 0, j)),
            pl.BlockSpec((None, 1, D_MODEL), lambda l, j: (l, 0, j)),
        ],
        out_specs=pl.BlockSpec((None, n, D_MODEL), lambda l, j: (l, 0, j)),
        out_shape=jax.ShapeDtypeStruct((DEPTH, n, 3 * D_MODEL), F32),
        compiler_params=_params("arbitrary", "arbitrary"),
        name="modulation",
    )(cond, w_mod, b_mod.reshape(DEPTH, 1, 3 * D_MODEL))


def _ab_in_kernel(x_ref, shift_ref, scale_ref, w_ref, lng_ref, lnb_ref, ws_ref, bs_ref,
                  lx_ref, ga_ref, yb_ref, *, tm):
    h = (x_ref[...] * (1.0 + scale_ref[...]) + shift_ref[...]).astype(BF16)

    def proj(j):
        return _dot(h, w_ref[:, j * W:(j + 1) * W])

    lx_ref[...] = proj(0)
    ga_ref[...] = _silu(proj(1)).astype(BF16)
    u = jax.nn.gelu(proj(2))
    v = jax.nn.gelu(proj(3))
    mu = jnp.mean(v, axis=-1, keepdims=True)
    vc = v - mu
    var = jnp.mean(vc * vc, axis=-1, keepdims=True)
    vn = ((vc * lax.rsqrt(var + LN_EPS)) * lng_ref[...] + lnb_ref[...]).astype(BF16)
    ug = u * _silu(proj(4))
    for c in range(tm // SGU_CHUNK):
        rows = slice(c * SGU_CHUNK, (c + 1) * SGU_CHUNK)
        for g in range(SGU_GROUPS):
            cols = slice(g * LANES, (g + 1) * LANES)
            s = _dot(ws_ref[g], vn[rows, cols]) + bs_ref[g]
            yb_ref[rows, cols] = (ug[rows, cols] * s).astype(BF16)


def _ab_in(x, shift, scale, w_in, sgu_ln_g, sgu_ln_b, sgu_w, sgu_b, tm):
    bsz, n, _ = x.shape
    tok = lambda width: pl.BlockSpec((None, tm, width), lambda b, i: (b, i, 0))
    consts = [w_in, sgu_ln_g, sgu_ln_b, sgu_w, sgu_b]
    return pl.pallas_call(
        functools.partial(_ab_in_kernel, tm=tm),
        grid=(bsz, n // tm),
        in_specs=[tok(D_MODEL), _mod_spec(shift), _mod_spec(scale)] + [_bcast_spec(a, 2) for a in consts],
        out_specs=[tok(W), tok(W), tok(W)],
        out_shape=[jax.ShapeDtypeStruct((bsz, n, W), F32),
                   jax.ShapeDtypeStruct((bsz, n, W), BF16),
                   jax.ShapeDtypeStruct((bsz, n, W), BF16)],
        compiler_params=_params("parallel", "parallel"),
        name="ab_in_sgu",
    )(x, shift, scale, *consts)


def _lru_kernel(lx_ref, lxc_ref, ga_ref, gac_ref, cw_ref, cb_ref, wp_ref, ba_ref, bx_ref, lam_ref,
                ya_ref, yac_ref, xp_ref, xpc_ref, af_ref, bf_ref, ar_ref, br_ref, *, n, nc, tr):
    nt = n + nc
    pad = SUBLANES
    zeros = jnp.zeros((pad, W), F32)
    for ref, src, length in ((xp_ref, lx_ref, n), (xpc_ref, lxc_ref, nc)):
        ref[0:pad, :] = zeros
        ref[pad + length:2 * pad + length, :] = zeros
        ref[pad:pad + length, :] = src[...]

    lam = lam_ref[...]
    neg = -lam
    softplus = jnp.maximum(neg, 0.0) + jnp.log1p(jnp.exp(-jnp.abs(neg)))
    cw = cw_ref[...]
    cb = cb_ref[...]

    def coeffs(src_ref, r0, f_off, r_off):
        xw = src_ref[pl.ds(r0, tr + 2 * pad), :]
        xc = cb + sum(cw[k:k + 1, :] * xw[pad - 2 + k:pad - 2 + k + tr, :] for k in range(4))
        xcb = xc.astype(BF16)
        for j in range(N_PAIR):
            cols = slice(j * LANES, (j + 1) * LANES)
            gts = _dot(xcb[:, cols], wp_ref[j])
            xj = xc[:, cols]
            for d, (a_ref, b_ref, off) in enumerate(((af_ref, bf_ref, f_off), (ar_ref, br_ref, r_off))):
                gate_r = jax.nn.sigmoid(gts[:, (2 * d) * LANES:(2 * d + 1) * LANES] + ba_ref[d:d + 1, cols])
                gate_i = jax.nn.sigmoid(gts[:, (2 * d + 1) * LANES:(2 * d + 2) * LANES] + bx_ref[d:d + 1, cols])
                log_a = (-LRU_C * gate_r) * softplus[d:d + 1, cols]
                a = jnp.exp(log_a)
                one_minus_a2 = jnp.tanh(-log_a) * (1.0 + a * a)
                a_ref[pl.ds(off + r0, tr), cols] = a
                b_ref[pl.ds(off + r0, tr), cols] = jnp.sqrt(one_minus_a2) * (gate_i * xj)

    def lat_tile(i, carry):
        coeffs(xp_ref, pl.multiple_of(i * tr, tr), nc, 0)
        return carry

    lax.fori_loop(0, n // tr, lat_tile, 0)
    for i in range(nc // tr):
        coeffs(xpc_ref, i * tr, 0, n)

    def step(t, carry):
        hf, hr = carry
        rowf = pl.ds(t, 1)
        hf = af_ref[rowf, :] * hf + bf_ref[rowf, :]
        bf_ref[rowf, :] = hf
        rowr = pl.ds(nt - 1 - t, 1)
        hr = ar_ref[rowr, :] * hr + br_ref[rowr, :]
        br_ref[rowr, :] = hr
        return hf, hr

    h0 = jnp.zeros((1, W), F32)
    lax.fori_loop(0, nt, step, (h0, h0), unroll=8)

    def out_tile(i, carry):
        r0 = pl.multiple_of(i * tr, tr)
        h = bf_ref[pl.ds(nc + r0, tr), :] + br_ref[pl.ds(r0, tr), :]
        ya_ref[pl.ds(r0, tr), :] = (h * ga_ref[pl.ds(r0, tr), :].astype(F32)).astype(BF16)
        return carry

    lax.fori_loop(0, n // tr, out_tile, 0)
    yac_ref[...] = ((bf_ref[0:nc, :] + br_ref[n:nt, :]) * gac_ref[...].astype(F32)).astype(BF16)


def _lru(lx, lxc, ga, gac, conv_w, conv_b, wpair, b_a, b_x, lam, tr=256):
    bsz, n, _ = lx.shape
    nc = lxc.shape[1]
    seq = lambda length: pl.BlockSpec((None, length, W), lambda b: (b, 0, 0))
    consts = [conv_w, conv_b, wpair, b_a, b_x, lam]
    pad = 2 * SUBLANES
    return pl.pallas_call(
        functools.partial(_lru_kernel, n=n, nc=nc, tr=tr),
        grid=(bsz,),
        in_specs=[seq(n), seq(nc), seq(n), seq(nc)] + [_bcast_spec(a, 1) for a in consts],
        out_specs=[seq(n), seq(nc)],
        out_shape=[jax.ShapeDtypeStruct((bsz, n, W), BF16), jax.ShapeDtypeStruct((bsz, nc, W), BF16)],
        scratch_shapes=[pltpu.VMEM((n + pad, W), F32), pltpu.VMEM((nc + pad, W), F32)]
        + [pltpu.VMEM((n + nc, W), F32)] * 4,
        compiler_params=_params("parallel"),
        name="rglru",
    )(lx, lxc, ga, gac, *consts)


def _residual_ln(x, y, gate, g, b):
    z = DEEPNORM_ALPHA * x + gate * y
    mu = jnp.mean(z, axis=-1, keepdims=True)
    zc = z - mu
    var = jnp.mean(zc * zc, axis=-1, keepdims=True)
    return (zc * lax.rsqrt(var + LN_EPS)) * g + b


def _ab_out_kernel(x_ref, ya_ref, yb_ref, gate_ref, w_ref, g_ref, b_ref, o_ref):
    y = _dot(ya_ref[...], w_ref[0:W, :]) + _dot(yb_ref[...], w_ref[W:2 * W, :])
    o_ref[...] = _residual_ln(x_ref[...], y, gate_ref[...], g_ref[...], b_ref[...])


def _ab_out(x, ya, yb, gate, w_out, ln_g, ln_b, tm):
    bsz, n, _ = x.shape
    tok = lambda width: pl.BlockSpec((None, tm, width), lambda b, i: (b, i, 0))
    consts = [w_out, ln_g, ln_b]
    return pl.pallas_call(
        _ab_out_kernel,
        grid=(bsz, n // tm),
        in_specs=[tok(D_MODEL), tok(W), tok(W), _mod_spec(gate)] + [_bcast_spec(a, 2) for a in consts],
        out_specs=tok(D_MODEL),
        out_shape=jax.ShapeDtypeStruct((bsz, n, D_MODEL), F32),
        compiler_params=_params("parallel", "parallel"),
        name="ab_out_ln",
    )(x, ya, yb, gate, *consts)


def _rope(t, cos, sin_signed, first_half):
    parts = []
    for j in range(W // LANES):
        tj = t[:, j * LANES:(j + 1) * LANES]
        parts.append(jnp.where(first_half, pltpu.roll(tj, LANES - 16, 1), pltpu.roll(tj, 16, 1)))
    return t * cos + jnp.concatenate(parts, axis=1) * sin_signed


def _cd_in_kernel(x_ref, shift_ref, scale_ref, w_ref, cos_ref, sin_ref,
                  z_ref, cg_ref, q_ref, k_ref, v_ref, ag_ref, *, tm):
    h = (x_ref[...] * (1.0 + scale_ref[...]) + shift_ref[...]).astype(BF16)

    def proj(j):
        return _dot(h, w_ref[:, j * W:(j + 1) * W])

    z_ref[...] = proj(2) * proj(0)
    cg_ref[...] = (proj(1) * _silu(proj(3))).astype(BF16)
    cos = cos_ref[...]
    sin = sin_ref[...]
    lane = lax.broadcasted_iota(jnp.int32, (tm, LANES), 1)
    first_half = (lane % 32) < 16
    q_ref[...] = (_rope(proj(4), cos, sin, first_half) * (DIFF_HEAD_DIM ** -0.5)).astype(BF16)
    k_ref[...] = _rope(proj(5), cos, sin, first_half).astype(BF16)
    v_ref[...] = proj(6).astype(BF16)
    ag_ref[...] = _silu(proj(7)).astype(BF16)


def _cd_in(x, shift, scale, w_in, cos, sin_signed, tm):
    bsz, n, _ = x.shape
    tok = lambda width: pl.BlockSpec((None, tm, width), lambda i, b: (b, i, 0))
    mod = lambda arr: pl.BlockSpec((None, 1, D_MODEL), lambda i, b: (b, 0, 0))
    table = pl.BlockSpec((tm, W), lambda i, b: (i, 0))
    return pl.pallas_call(
        functools.partial(_cd_in_kernel, tm=tm),
        grid=(n // tm, bsz),
        in_specs=[tok(D_MODEL), mod(shift), mod(scale), _bcast_spec(w_in, 2), table, table],
        out_specs=[tok(W)] * 6,
        out_shape=[jax.ShapeDtypeStruct((bsz, n, W), F32)] + [jax.ShapeDtypeStruct((bsz, n, W), BF16)] * 5,
        compiler_params=_params("parallel", "parallel"),
        name="cd_in_rope",
    )(x, shift, scale, w_in, cos, sin_signed)


def _ctx_kv_kernel(x_ref, shift_ref, scale_ref, w_ref, k_ref, v_ref):
    h = (x_ref[...] * (1.0 + scale_ref[...]) + shift_ref[...]).astype(BF16)
    k_ref[...] = _dot(h, w_ref[:, 0:W]).astype(BF16)
    v_ref[...] = _dot(h, w_ref[:, W:2 * W]).astype(BF16)


def _ctx_kv(ctx, shift, scale, w_kv, tm):
    bsz, n, _ = ctx.shape
    tok = lambda width: pl.BlockSpec((None, tm, width), lambda b, i: (b, i, 0))
    return pl.pallas_call(
        _ctx_kv_kernel,
        grid=(bsz, n // tm),
        in_specs=[tok(D_MODEL), _mod_spec(shift), _mod_spec(scale), _bcast_spec(w_kv, 2)],
        out_specs=[tok(W), tok(W)],
        out_shape=[jax.ShapeDtypeStruct((bsz, n, W), BF16)] * 2,
        compiler_params=_params("parallel", "parallel"),
        name="ctx_kv",
    )(ctx, shift, scale, w_kv)


def _attn_kernel(q_ref, kc_ref, k_ref, vc_ref, v_ref, ag_ref, dl_ref, g_ref, o_ref, *, tq):
    dl = dl_ref[...]
    lam = (jnp.exp(jnp.sum(dl[0:1, :] * dl[1:2, :], axis=-1, keepdims=True))
           - jnp.exp(jnp.sum(dl[2:3, :] * dl[3:4, :], axis=-1, keepdims=True)) + LAM_INIT_1)
    q = q_ref[...]
    lane = lax.broadcasted_iota(jnp.int32, (tq, DIFF_V_DIM), 1)
    zero = jnp.zeros_like(q)
    qq = jnp.concatenate([jnp.where(lane < DIFF_HEAD_DIM, q, zero), jnp.where(lane >= DIFF_HEAD_DIM, q, zero)], axis=0)
    nt = (((1,), (1,)), ((), ()))
    s_c = lax.dot_general(qq, kc_ref[...], nt, preferred_element_type=F32)
    s_l = lax.dot_general(qq, k_ref[...], nt, preferred_element_type=F32)
    m = jnp.maximum(jnp.max(s_c, axis=-1, keepdims=True), jnp.max(s_l, axis=-1, keepdims=True))
    p_c = jnp.exp(s_c - m)
    p_l = jnp.exp(s_l - m)
    r = 1.0 / (jnp.sum(p_c, axis=-1, keepdims=True) + jnp.sum(p_l, axis=-1, keepdims=True))
    r0 = r[0:tq, :]
    r1 = lam * r[tq:2 * tq, :]
    a_c = (p_c[0:tq, :] * r0 - p_c[tq:2 * tq, :] * r1).astype(BF16)
    a_l = (p_l[0:tq, :] * r0 - p_l[tq:2 * tq, :] * r1).astype(BF16)
    o = _dot(a_c, vc_ref[...]) + _dot(a_l, v_ref[...])
    o = (o * lax.rsqrt(jnp.mean(o * o, axis=-1, keepdims=True) + LN_EPS)) * g_ref[...] * (1.0 - LAM_INIT_1)
    o_ref[...] = (o * ag_ref[...].astype(F32)).astype(BF16)


def _attention(q, kc, k, vc, v, ag, diff_lambda, subln_g, tq):
    bsz, n, _ = q.shape
    nc = kc.shape[1]
    qblk = pl.BlockSpec((None, tq, DIFF_V_DIM), lambda b, h, i: (b, i, h))
    keys = lambda length: pl.BlockSpec((None, length, DIFF_V_DIM), lambda b, h, i: (b, 0, h))
    consts = [diff_lambda, subln_g]
    return pl.pallas_call(
        functools.partial(_attn_kernel, tq=tq),
        grid=(bsz, DIFF_HEADS, n // tq),
        in_specs=[qblk, keys(nc), keys(n), keys(nc), keys(n), qblk] + [_bcast_spec(a, 3) for a in consts],
        out_specs=qblk,
        out_shape=jax.ShapeDtypeStruct((bsz, n, W), BF16),
        compiler_params=_params("parallel", "parallel", "parallel"),
        name="diff_attention",
    )(q, kc, k, vc, v, ag, *consts)


def _cd_out_kernel(x_ref, z_ref, zprev_ref, znext_ref, cg_ref, yd_ref, gate_ref, cw_ref, w_ref, g_ref, b_ref,
                   o_ref, *, tm):
    i = pl.program_id(1)
    z = z_ref[...]
    row = lax.broadcasted_iota(jnp.int32, (tm, W), 0)
    prev_row = jnp.where(i > 0, zprev_ref[SUBLANES - 1:SUBLANES, :], 0.0)
    next_row = jnp.where(i < pl.num_programs(1) - 1, znext_ref[0:1, :], 0.0)
    z_m1 = jnp.where(row == 0, prev_row, pltpu.roll(z, 1, 0))
    z_p1 = jnp.where(row == tm - 1, next_row, pltpu.roll(z, tm - 1, 0))
    cw = cw_ref[...]
    conv = cw[0:1, :] * z_m1 + cw[1:2, :] * z + cw[2:3, :] * z_p1
    yc = (cg_ref[...].astype(F32) * conv).astype(BF16)
    y = _dot(yc, w_ref[0:W, :]) + _dot(yd_ref[...], w_ref[W:2 * W, :])
    o_ref[...] = _residual_ln(x_ref[...], y, gate_ref[...], g_ref[...], b_ref[...])


def _cd_out(x, z, cg, yd, gate, sconv_w, w_out, ln_g, ln_b, tm):
    bsz, n, _ = x.shape
    tok = lambda width: pl.BlockSpec((None, tm, width), lambda b, i: (b, i, 0))
    per = tm // SUBLANES
    last = n // SUBLANES - 1
    zprev = pl.BlockSpec((None, SUBLANES, W), lambda b, i: (b, jnp.maximum(i * per - 1, 0), 0))
    znext = pl.BlockSpec((None, SUBLANES, W), lambda b, i: (b, jnp.minimum((i + 1) * per, last), 0))
    consts = [sconv_w, w_out, ln_g, ln_b]
    return pl.pallas_call(
        functools.partial(_cd_out_kernel, tm=tm),
        grid=(bsz, n // tm),
        in_specs=[tok(D_MODEL), tok(W), zprev, znext, tok(W), tok(W), _mod_spec(gate)]
        + [_bcast_spec(a, 2) for a in consts],
        out_specs=tok(D_MODEL),
        out_shape=jax.ShapeDtypeStruct((bsz, n, D_MODEL), F32),
        compiler_params=_params("parallel", "arbitrary"),
        name="cd_out_ln",
    )(x, z, z, z, cg, yd, gate, *consts)


def _pair_gate_weights(w_a, w_x):
    def pair_blockdiag(w):
        wp = w.reshape(N_PAIR, 2, LRU_HEAD_DIM, LRU_HEAD_DIM)
        zero = jnp.zeros_like(wp[:, 0])
        top = jnp.concatenate([wp[:, 0], zero], axis=-1)
        bot = jnp.concatenate([zero, wp[:, 1]], axis=-1)
        return jnp.concatenate([top, bot], axis=-2)
    return jnp.concatenate([pair_blockdiag(w_a[0]), pair_blockdiag(w_x[0]),
                            pair_blockdiag(w_a[1]), pair_blockdiag(w_x[1])], axis=-1).astype(BF16)


def _rope_tables(n):
    rows = n // GRID_W
    row = jnp.repeat(jnp.arange(rows, dtype=F32), GRID_W)
    col = jnp.tile(jnp.arange(GRID_W, dtype=F32), rows)
    n_freq = DIFF_HEAD_DIM // 4
    inv = ROPE_BASE ** (-jnp.arange(n_freq, dtype=F32) / n_freq)
    ang_r = row[:, None] * inv
    ang_c = col[:, None] * inv
    ang = jnp.concatenate([ang_r, ang_r, ang_c, ang_c], axis=-1)
    sign = jnp.tile(jnp.concatenate([-jnp.ones((n_freq,), F32), jnp.ones((n_freq,), F32)]), 2)
    reps = W // DIFF_HEAD_DIM
    return jnp.tile(jnp.cos(ang), (1, reps)), jnp.tile(jnp.sin(ang) * sign, (1, reps))


def kernel(x, c, ctx, c_ctx, w_mod, b_mod, ln_g, ln_b, ab_w_in, ab_w_out, lru_conv_w, lru_conv_b, lru_w_a, lru_b_a,
           lru_w_x, lru_b_x, lru_lambda, sgu_ln_g, sgu_ln_b, sgu_w, sgu_b, cd_w_in, cd_w_out, sconv_w, diff_lambda,
           diff_subln_g):
    bsz, n, _ = x.shape
    nc = ctx.shape[1]
    row = lambda a: a.reshape(1, -1)

    n_cond = -(-(bsz + 1) // SUBLANES) * SUBLANES
    cond = jnp.zeros((n_cond, D_MODEL), F32).at[:bsz].set(c).at[bsz].set(c_ctx)
    mod = _modulation(cond, w_mod, b_mod)

    def mods(l):
        lat = [mod[l, :bsz, j * D_MODEL:(j + 1) * D_MODEL][:, None, :] for j in range(3)]
        con = [mod[l, bsz:bsz + 1, j * D_MODEL:(j + 1) * D_MODEL][:, None, :] for j in range(3)]
        return lat, con

    (shift, scale, gate), (shift_c, scale_c, gate_c) = mods(0)
    w_in = ab_w_in[0].astype(BF16)
    w_out = ab_w_out[0].astype(BF16)
    sgu_args = (row(sgu_ln_g[0]), row(sgu_ln_b[0]), sgu_w[0].astype(BF16),
                jnp.broadcast_to(sgu_b[0][:, :, None], (SGU_GROUPS, SGU_CHUNK, LANES)))
    lx, ga, yb = _ab_in(x, shift, scale, w_in, *sgu_args, tm=512)
    lxc, gac, ybc = _ab_in(ctx, shift_c, scale_c, w_in, *sgu_args, tm=nc)
    wpair = _pair_gate_weights(lru_w_a[0], lru_w_x[0])
    ya, yac = _lru(lx, lxc, ga, gac, lru_conv_w[0], row(lru_conv_b[0]), wpair, lru_b_a[0], lru_b_x[0], lru_lambda[0])
    x1 = _ab_out(x, ya, yb, gate, w_out, row(ln_g[0]), row(ln_b[0]), tm=512)
    ctx1 = _ab_out(ctx, yac, ybc, gate_c, w_out, row(ln_g[0]), row(ln_b[0]), tm=nc)

    (shift, scale, gate), (shift_c, scale_c, _) = mods(1)
    w_in = cd_w_in[0].astype(BF16)
    w_out = cd_w_out[0].astype(BF16)
    cos, sin_signed = _rope_tables(n)
    z, cg, q, k, v, ag = _cd_in(x1, shift, scale, w_in, cos, sin_signed, tm=512)
    kc, vc = _ctx_kv(ctx1, shift_c, scale_c, w_in[:, 5 * W:7 * W], tm=nc)
    yd = _attention(q, kc, k, vc, v, ag, diff_lambda[0], row(diff_subln_g[0]), tq=256)
    return _cd_out(x1, z, cg, yd, gate, sconv_w[0], w_out, row(ln_g[1]), row(ln_b[1]), tm=512)
```

```python
import functools
import math

import jax
import jax.numpy as jnp
from jax import lax
from jax.experimental import pallas as pl
from jax.experimental.pallas import tpu as pltpu

D_MODEL = 1024
DEPTH = 2
GRID_W = 64
W = D_MODEL // 2
LRU_HEAD_DIM = 64
LRU_C = 8.0
LRU_PAIR = 2 * LRU_HEAD_DIM
N_PAIR = W // LRU_PAIR
SGU_CHUNK = 128
SGU_GROUPS = 4
DIFF_V_DIM = 128
DIFF_HEADS = W // DIFF_V_DIM
DIFF_HEAD_DIM = DIFF_V_DIM // 2
ROPE_BASE = 10000.0
DEEPNORM_ALPHA = (2 * DEPTH) ** 0.25
LN_EPS = 1e-5
LAM_INIT_1 = 0.8 - 0.6 * math.exp(-0.3 * 1)
LOG2_E = math.log2(math.e)

SUBLANES = 8
LANES = 128
VMEM_LIMIT = 56 * 1024 * 1024

F32 = jnp.float32
BF16 = jnp.bfloat16


def _silu(x):
    return x * jax.nn.sigmoid(x)


def _dot(a, b):
    return jnp.dot(a, b, preferred_element_type=F32)


def _params(*sem):
    return pltpu.CompilerParams(dimension_semantics=sem, vmem_limit_bytes=VMEM_LIMIT)


def _bcast_spec(arr, nd_grid):
    zeros = (0,) * arr.ndim
    return pl.BlockSpec(arr.shape, lambda *_: zeros)


def _mod_spec(arr):
    if arr.shape[0] == 1:
        return pl.BlockSpec((None, 1, D_MODEL), lambda b, i: (0, 0, 0))
    return pl.BlockSpec((None, 1, D_MODEL), lambda b, i: (b, 0, 0))


def _mod_kernel(cond_ref, w_ref, b_ref, o_ref):
    s = _silu(cond_ref[...]).astype(BF16)
    o_ref[...] = _dot(s, w_ref[...].astype(BF16)) + b_ref[...]


def _modulation(cond, w_mod, b_mod):
    n = cond.shape[0]
    return pl.pallas_call(
        _mod_kernel,
        grid=(DEPTH, 3),
        in_specs=[
            pl.BlockSpec((n, D_MODEL), lambda l, j: (0, 0)),
            pl.BlockSpec((None, D_MODEL, D_MODEL), lambda l, j: (l, 0, j)),
            pl.BlockSpec((None, 1, D_MODEL), lambda l, j: (l, 0, j)),
        ],
        out_specs=pl.BlockSpec((None, n, D_MODEL), lambda l, j: (l, 0, j)),
        out_shape=jax.ShapeDtypeStruct((DEPTH, n, 3 * D_MODEL), F32),
        compiler_params=_params("arbitrary", "arbitrary"),
        name="modulation",
    )(cond, w_mod, b_mod.reshape(DEPTH, 1, 3 * D_MODEL))


def _ab_in_kernel(x_ref, shift_ref, scale_ref, w_ref, lng_ref, lnb_ref, ws_ref, bs_ref,
                  lx_ref, ga_ref, yb_ref, *, tm):
    h = (x_ref[...] * (1.0 + scale_ref[...]) + shift_ref[...]).astype(BF16)

    def proj(j):
        return _dot(h, w_ref[:, j * W:(j + 1) * W])

    lx_ref[...] = proj(0)
    ga_ref[...] = _silu(proj(1)).astype(BF16)
    u = jax.nn.gelu(proj(2))
    v = jax.nn.gelu(proj(3))
    mu = jnp.mean(v, axis=-1, keepdims=True)
    vc = v - mu
    var = jnp.mean(vc * vc, axis=-1, keepdims=True)
    vn = ((vc * lax.rsqrt(var + LN_EPS)) * lng_ref[...] + lnb_ref[...]).astype(BF16)
    ug = u * _silu(proj(4))
    for c in range(tm // SGU_CHUNK):
        rows = slice(c * SGU_CHUNK, (c + 1) * SGU_CHUNK)
        for g in range(SGU_GROUPS):
            cols = slice(g * LANES, (g + 1) * LANES)
            s = _dot(ws_ref[g], vn[rows, cols]) + bs_ref[g]
            yb_ref[rows, cols] = (ug[rows, cols] * s).astype(BF16)


def _ab_in(x, shift, scale, w_in, sgu_ln_g, sgu_ln_b, sgu_w, sgu_b, tm):
    bsz, n, _ = x.shape
    tok = lambda width: pl.BlockSpec((None, tm, width), lambda b, i: (b, i, 0))
    consts = [w_in, sgu_ln_g, sgu_ln_b, sgu_w, sgu_b]
    return pl.pallas_call(
        functools.partial(_ab_in_kernel, tm=tm),
        grid=(bsz, n // tm),
        in_specs=[tok(D_MODEL), _mod_spec(shift), _mod_spec(scale)] + [_bcast_spec(a, 2) for a in consts],
        out_specs=[tok(W), tok(W), tok(W)],
        out_shape=[jax.ShapeDtypeStruct((bsz, n, W), F32),
                   jax.ShapeDtypeStruct((bsz, n, W), BF16),
                   jax.ShapeDtypeStruct((bsz, n, W), BF16)],
        compiler_params=_params("parallel", "parallel"),
        name="ab_in_sgu",
    )(x, shift, scale, *consts)


SCAN_SEGMENTS = SUBLANES
SCAN_BLOCK = 32


def _lru_kernel(lx_ref, lxc_ref, ga_ref, gac_ref, cw_ref, cb_ref, wp_ref, ba_ref, bx_ref, lam_ref,
                ya_ref, yac_ref, xp_ref, xpc_ref, af_ref, bf_ref, ar_ref, br_ref, hf_ref, hr_ref, *, n, nc, tr):
    nt = n + nc
    lseg = nt // SCAN_SEGMENTS
    pad = SUBLANES
    wl = lx_ref.shape[-1]
    npair = wl // LANES
    zeros = jnp.zeros((pad, wl), F32)
    for ref, src, length in ((xp_ref, lx_ref, n), (xpc_ref, lxc_ref, nc)):
        ref[0:pad, :] = zeros
        ref[pad + length:2 * pad + length, :] = zeros
        ref[pad:pad + length, :] = src[...]

    lam = lam_ref[...]
    neg = -lam
    softplus = jnp.maximum(neg, 0.0) + jnp.log1p(jnp.exp(-jnp.abs(neg)))
    cw = cw_ref[...]
    cb = cb_ref[...]

    def block(ref, j, pos):
        if isinstance(pos, int):
            seg, step = divmod(pos, lseg)
        else:
            seg = lax.div(pos, lseg)
            step = pos - seg * lseg
        return ref.at[j, pl.ds(step * SCAN_SEGMENTS + seg, SCAN_BLOCK, stride=SCAN_SEGMENTS), :]

    def coeffs(src_ref, r0, f_off, r_off):
        xw = src_ref[pl.ds(r0, tr + 2 * pad), :]
        xc = cb + sum(cw[k:k + 1, :] * xw[pad - 2 + k:pad - 2 + k + tr, :] for k in range(4))
        xcb = xc.astype(BF16)
        for j in range(npair):
            cols = slice(j * LANES, (j + 1) * LANES)
            gts = _dot(xcb[:, cols], wp_ref[j])
            xj = xc[:, cols]
            for d, (a_ref, b_ref, off) in enumerate(((af_ref, bf_ref, f_off), (ar_ref, br_ref, r_off))):
                gate_r = jax.nn.sigmoid(gts[:, (2 * d) * LANES:(2 * d + 1) * LANES] + ba_ref[d:d + 1, cols])
                gate_i = jax.nn.sigmoid(gts[:, (2 * d + 1) * LANES:(2 * d + 2) * LANES] + bx_ref[d:d + 1, cols])
                log_a = (-LRU_C * gate_r) * softplus[d:d + 1, cols]
                a = jnp.exp(log_a)
                one_minus_a2 = jnp.tanh(-log_a) * (1.0 + a * a)
                b = jnp.sqrt(one_minus_a2) * (gate_i * xj)
                for blk in range(0, tr, SCAN_BLOCK):
                    block(a_ref, j, off + r0 + blk)[...] = a[blk:blk + SCAN_BLOCK, :]
                    block(b_ref, j, off + r0 + blk)[...] = b[blk:blk + SCAN_BLOCK, :]

    def lat_tile(i, carry):
        coeffs(xp_ref, pl.multiple_of(i * tr, tr), nc, 0)
        return carry

    lax.fori_loop(0, n // tr, lat_tile, 0)
    for i in range(nc // tr):
        coeffs(xpc_ref, i * tr, 0, n)

    chains = [(a_ref, b_ref, h_ref, j, rev)
              for a_ref, b_ref, h_ref, rev in ((af_ref, bf_ref, hf_ref, False), (ar_ref, br_ref, hr_ref, True))
              for j in range(npair)]

    def step_rows(t, rev):
        return pl.ds(pl.multiple_of((lseg - 1 - t if rev else t) * SCAN_SEGMENTS, SCAN_SEGMENTS), SCAN_SEGMENTS)

    def totals_step(t, carry):
        out = []
        for (a_ref, b_ref, _, j, rev), (decay, h) in zip(chains, carry):
            rows = step_rows(t, rev)
            a = a_ref[j, rows, :]
            out.append((a * decay, a * h + b_ref[j, rows, :]))
        return tuple(out)

    one = jnp.ones((SCAN_SEGMENTS, LANES), F32)
    zero = jnp.zeros((SCAN_SEGMENTS, LANES), F32)
    totals = lax.fori_loop(0, lseg, totals_step, tuple((one, zero) for _ in chains), unroll=8)

    def entering_state(decay, h_end, rev):
        order = range(SCAN_SEGMENTS - 1, -1, -1) if rev else range(SCAN_SEGMENTS)
        rows = [None] * SCAN_SEGMENTS
        state = jnp.zeros((1, LANES), F32)
        for s in order:
            rows[s] = state
            state = decay[s:s + 1, :] * state + h_end[s:s + 1, :]
        return jnp.concatenate(rows, axis=0)

    starts = tuple(entering_state(decay, h_end, chain[-1]) for chain, (decay, h_end) in zip(chains, totals))

    def states_step(t, carry):
        out = []
        for (a_ref, b_ref, h_ref, j, rev), h in zip(chains, carry):
            rows = step_rows(t, rev)
            h = a_ref[j, rows, :] * h + b_ref[j, rows, :]
            h_ref[j, rows, :] = h
            out.append(h)
        return tuple(out)

    lax.fori_loop(0, lseg, states_step, starts, unroll=8)

    def write_out(y_ref, g_ref, r0, f_pos, r_pos, rows):
        for blk in range(0, rows, SCAN_BLOCK):
            dst = pl.ds(r0 + blk, SCAN_BLOCK)
            for j in range(npair):
                cols = slice(j * LANES, (j + 1) * LANES)
                h = block(hf_ref, j, f_pos + blk)[...] + block(hr_ref, j, r_pos + blk)[...]
                y_ref[dst, cols] = (h * g_ref[dst, cols].astype(F32)).astype(BF16)

    def out_tile(i, carry):
        r0 = pl.multiple_of(i * tr, tr)
        write_out(ya_ref, ga_ref, r0, nc + r0, r0, tr)
        return carry

    lax.fori_loop(0, n // tr, out_tile, 0)
    write_out(yac_ref, gac_ref, 0, 0, n, nc)


def _lru(lx, lxc, ga, gac, conv_w, conv_b, wpair, b_a, b_x, lam, tr=256):
    bsz, n, _ = lx.shape
    nc = lxc.shape[1]
    lseg, rem = divmod(n + nc, SCAN_SEGMENTS)
    assert rem == 0 and lseg % SCAN_BLOCK == 0 and n % tr == 0 and nc % tr == 0 and tr % SCAN_BLOCK == 0
    npair = 2
    wl = npair * LANES
    seq = lambda length: pl.BlockSpec((None, length, wl), lambda b, g: (b, 0, g))
    vec = lambda rows: pl.BlockSpec((rows, wl), lambda b, g: (0, g))
    pad = 2 * SUBLANES
    coeff = pltpu.VMEM((npair, SCAN_SEGMENTS * lseg, LANES), F32)
    return pl.pallas_call(
        functools.partial(_lru_kernel, n=n, nc=nc, tr=tr),
        grid=(bsz, N_PAIR // npair),
        in_specs=[seq(n), seq(nc), seq(n), seq(nc), vec(conv_w.shape[0]), vec(1),
                  pl.BlockSpec((npair,) + wpair.shape[1:], lambda b, g: (g, 0, 0)), vec(2), vec(2), vec(2)],
        out_specs=[seq(n), seq(nc)],
        out_shape=[jax.ShapeDtypeStruct((bsz, n, W), BF16), jax.ShapeDtypeStruct((bsz, nc, W), BF16)],
        scratch_shapes=[pltpu.VMEM((n + pad, wl), F32), pltpu.VMEM((nc + pad, wl), F32)] + [coeff] * 6,
        compiler_params=_params("parallel", "parallel"),
        name="rglru",
    )(lx, lxc, ga, gac, conv_w, conv_b, wpair, b_a, b_x, lam)


def _residual_ln(x, y, gate, g, b):
    z = DEEPNORM_ALPHA * x + gate * y
    mu = jnp.mean(z, axis=-1, keepdims=True)
    zc = z - mu
    var = jnp.mean(zc * zc, axis=-1, keepdims=True)
    return (zc * lax.rsqrt(var + LN_EPS)) * g + b


def _ab_out_kernel(x_ref, ya_ref, yb_ref, gate_ref, w_ref, g_ref, b_ref, o_ref):
    y = _dot(ya_ref[...], w_ref[0:W, :]) + _dot(yb_ref[...], w_ref[W:2 * W, :])
    o_ref[...] = _residual_ln(x_ref[...], y, gate_ref[...], g_ref[...], b_ref[...])


def _ab_out(x, ya, yb, gate, w_out, ln_g, ln_b, tm):
    bsz, n, _ = x.shape
    tok = lambda width: pl.BlockSpec((None, tm, width), lambda b, i: (b, i, 0))
    consts = [w_out, ln_g, ln_b]
    return pl.pallas_call(
        _ab_out_kernel,
        grid=(bsz, n // tm),
        in_specs=[tok(D_MODEL), tok(W), tok(W), _mod_spec(gate)] + [_bcast_spec(a, 2) for a in consts],
        out_specs=tok(D_MODEL),
        out_shape=jax.ShapeDtypeStruct((bsz, n, D_MODEL), F32),
        compiler_params=_params("parallel", "parallel"),
        name="ab_out_ln",
    )(x, ya, yb, gate, *consts)


def _rope(t, cos, sin_signed, first_half):
    parts = []
    for j in range(W // LANES):
        tj = t[:, j * LANES:(j + 1) * LANES]
        parts.append(jnp.where(first_half, pltpu.roll(tj, LANES - 16, 1), pltpu.roll(tj, 16, 1)))
    return t * cos + jnp.concatenate(parts, axis=1) * sin_signed


def _cd_in_kernel(x_ref, shift_ref, scale_ref, w_ref, cos_ref, sin_ref,
                  z_ref, cg_ref, q_ref, k_ref, v_ref, ag_ref, *, tm):
    h = (x_ref[...] * (1.0 + scale_ref[...]) + shift_ref[...]).astype(BF16)

    def proj(j):
        return _dot(h, w_ref[:, j * W:(j + 1) * W])

    z_ref[...] = proj(2) * proj(0)
    cg_ref[...] = (proj(1) * _silu(proj(3))).astype(BF16)
    cos = cos_ref[...]
    sin = sin_ref[...]
    lane = lax.broadcasted_iota(jnp.int32, (tm, LANES), 1)
    first_half = (lane % 32) < 16
    q_ref[...] = (_rope(proj(4), cos, sin, first_half) * (DIFF_HEAD_DIM ** -0.5 * LOG2_E)).astype(BF16)
    k_ref[...] = _rope(proj(5), cos, sin, first_half).astype(BF16)
    v_ref[...] = proj(6).astype(BF16)
    ag_ref[...] = _silu(proj(7)).astype(BF16)


def _cd_in(x, shift, scale, w_in, cos, sin_signed, tm):
    bsz, n, _ = x.shape
    tok = lambda width: pl.BlockSpec((None, tm, width), lambda i, b: (b, i, 0))
    mod = lambda arr: pl.BlockSpec((None, 1, D_MODEL), lambda i, b: (b, 0, 0))
    table = pl.BlockSpec((tm, W), lambda i, b: (i, 0))
    return pl.pallas_call(
        functools.partial(_cd_in_kernel, tm=tm),
        grid=(n // tm, bsz),
        in_specs=[tok(D_MODEL), mod(shift), mod(scale), _bcast_spec(w_in, 2), table, table],
        out_specs=[tok(W)] * 6,
        out_shape=[jax.ShapeDtypeStruct((bsz, n, W), F32)] + [jax.ShapeDtypeStruct((bsz, n, W), BF16)] * 5,
        compiler_params=_params("parallel", "parallel"),
        name="cd_in_rope",
    )(x, shift, scale, w_in, cos, sin_signed)


def _ctx_kv_kernel(x_ref, shift_ref, scale_ref, w_ref, k_ref, v_ref):
    h = (x_ref[...] * (1.0 + scale_ref[...]) + shift_ref[...]).astype(BF16)
    k_ref[...] = _dot(h, w_ref[:, 0:W]).astype(BF16)
    v_ref[...] = _dot(h, w_ref[:, W:2 * W]).astype(BF16)


def _ctx_kv(ctx, shift, scale, w_kv, tm):
    bsz, n, _ = ctx.shape
    tok = lambda width: pl.BlockSpec((None, tm, width), lambda b, i: (b, i, 0))
    return pl.pallas_call(
        _ctx_kv_kernel,
        grid=(bsz, n // tm),
        in_specs=[tok(D_MODEL), _mod_spec(shift), _mod_spec(scale), _bcast_spec(w_kv, 2)],
        out_specs=[tok(W), tok(W)],
        out_shape=[jax.ShapeDtypeStruct((bsz, n, W), BF16)] * 2,
        compiler_params=_params("parallel", "parallel"),
        name="ctx_kv",
    )(ctx, shift, scale, w_kv)


def _attn_kernel(q_ref, kc_ref, k_ref, vc_ref, v_ref, ag_ref, dl_ref, g_ref, o_ref,
                 kall_ref, vall_ref, s_ref, p_ref, al_ref, m_ref, acc_ref, *, tq, ck):
    nc = kc_ref.shape[0]
    n = k_ref.shape[0]
    nck = (n + nc) // ck

    @pl.when(pl.program_id(2) == 0)
    def _():
        kall_ref[0:n, :] = k_ref[...]
        kall_ref[n:n + nc, :] = kc_ref[...]
        vall_ref[0:n, 0:DIFF_V_DIM] = v_ref[...]
        vall_ref[n:n + nc, 0:DIFF_V_DIM] = vc_ref[...]
        vall_ref[:, DIFF_V_DIM:2 * DIFF_V_DIM] = jnp.ones((n + nc, DIFF_V_DIM), BF16)

    dl = dl_ref[...]
    lam = (jnp.exp(jnp.sum(dl[0:1, :] * dl[1:2, :], axis=-1, keepdims=True))
           - jnp.exp(jnp.sum(dl[2:3, :] * dl[3:4, :], axis=-1, keepdims=True)) + LAM_INIT_1)
    q = q_ref[...]
    lane = lax.broadcasted_iota(jnp.int32, (tq, DIFF_V_DIM), 1)
    zero = jnp.zeros_like(q)
    qq = jnp.concatenate([jnp.where(lane < DIFF_HEAD_DIM, q, zero), jnp.where(lane >= DIFF_HEAD_DIM, q, zero)], axis=0)
    nt = (((1,), (1,)), ((), ()))

    def scores(j, slot):
        kj = kall_ref[pl.ds(pl.multiple_of(j * ck, ck), ck), :]
        s_ref[slot] = lax.dot_general(qq, kj, nt, preferred_element_type=F32)

    def lanes(t):
        return jnp.concatenate([t] * (ck // LANES), axis=1)

    def softmax(slot, first):
        s = s_ref[slot]
        sm = s[:, 0:LANES]
        for c in range(1, ck // LANES):
            sm = jnp.maximum(sm, s[:, c * LANES:(c + 1) * LANES])
        m_new = jnp.broadcast_to(jnp.max(sm, axis=-1, keepdims=True), (2 * tq, LANES))
        if not first:
            m_old = m_ref[...]
            m_new = jnp.maximum(m_old, m_new)
            al_ref[slot] = jnp.exp2(m_old - m_new)
        m_ref[...] = m_new
        p_ref[slot] = jnp.exp2(s - lanes(m_new)).astype(BF16)

    def weighted_values(j, slot, first):
        vj = vall_ref[pl.ds(pl.multiple_of(j * ck, ck), ck), :]
        pv = _dot(p_ref[slot], vj)
        if first:
            acc_ref[...] = pv
        else:
            alpha = al_ref[slot]
            acc_ref[...] = jnp.concatenate([alpha, alpha], axis=1) * acc_ref[...] + pv

    slots = s_ref.shape[0]
    for t in range(nck + 2):
        if t < nck:
            scores(t, t % slots)
        if 1 <= t <= nck:
            softmax((t - 1) % slots, t == 1)
        if t >= 2:
            weighted_values(t - 2, (t - 2) % slots, t == 2)

    acc = acc_ref[:, 0:DIFF_V_DIM]
    w = 1.0 / acc_ref[:, DIFF_V_DIM:2 * DIFF_V_DIM]
    o = acc[0:tq, :] * w[0:tq, :] - acc[tq:2 * tq, :] * (lam * w[tq:2 * tq, :])
    o = (o * lax.rsqrt(jnp.mean(o * o, axis=-1, keepdims=True) + LN_EPS)) * g_ref[...] * (1.0 - LAM_INIT_1)
    o_ref[...] = (o * ag_ref[...].astype(F32)).astype(BF16)


def _attention(q, kc, k, vc, v, ag, diff_lambda, subln_g, tq, ck):
    bsz, n, _ = q.shape
    nc = kc.shape[1]
    qblk = pl.BlockSpec((None, tq, DIFF_V_DIM), lambda b, h, i: (b, i, h))
    keys = lambda length: pl.BlockSpec((None, length, DIFF_V_DIM), lambda b, h, i: (b, 0, h))
    consts = [diff_lambda, subln_g]
    slots = 2
    return pl.pallas_call(
        functools.partial(_attn_kernel, tq=tq, ck=ck),
        grid=(bsz, DIFF_HEADS, n // tq),
        in_specs=[qblk, keys(nc), keys(n), keys(nc), keys(n), qblk] + [_bcast_spec(a, 3) for a in consts],
        out_specs=qblk,
        out_shape=jax.ShapeDtypeStruct((bsz, n, W), BF16),
        scratch_shapes=[pltpu.VMEM((n + nc, DIFF_V_DIM), BF16), pltpu.VMEM((n + nc, 2 * DIFF_V_DIM), BF16),
                        pltpu.VMEM((slots, 2 * tq, ck), F32), pltpu.VMEM((slots, 2 * tq, ck), BF16),
                        pltpu.VMEM((slots, 2 * tq, LANES), F32), pltpu.VMEM((2 * tq, LANES), F32),
                        pltpu.VMEM((2 * tq, 2 * DIFF_V_DIM), F32)],
        compiler_params=_params("parallel", "parallel", "arbitrary"),
        name="diff_attention",
    )(q, kc, k, vc, v, ag, *consts)


def _cd_out_kernel(x_ref, z_ref, zprev_ref, znext_ref, cg_ref, yd_ref, gate_ref, cw_ref, w_ref, g_ref, b_ref,
                   o_ref, *, tm):
    i = pl.program_id(1)
    z = z_ref[...]
    row = lax.broadcasted_iota(jnp.int32, (tm, W), 0)
    prev_row = jnp.where(i > 0, zprev_ref[SUBLANES - 1:SUBLANES, :], 0.0)
    next_row = jnp.where(i < pl.num_programs(1) - 1, znext_ref[0:1, :], 0.0)
    z_m1 = jnp.where(row == 0, prev_row, pltpu.roll(z, 1, 0))
    z_p1 = jnp.where(row == tm - 1, next_row, pltpu.roll(z, tm - 1, 0))
    cw = cw_ref[...]
    conv = cw[0:1, :] * z_m1 + cw[1:2, :] * z + cw[2:3, :] * z_p1
    yc = (cg_ref[...].astype(F32) * conv).astype(BF16)
    y = _dot(yc, w_ref[0:W, :]) + _dot(yd_ref[...], w_ref[W:2 * W, :])
    o_ref[...] = _residual_ln(x_ref[...], y, gate_ref[...], g_ref[...], b_ref[...])


def _cd_out(x, z, cg, yd, gate, sconv_w, w_out, ln_g, ln_b, tm):
    bsz, n, _ = x.shape
    tok = lambda width: pl.BlockSpec((None, tm, width), lambda b, i: (b, i, 0))
    per = tm // SUBLANES
    last = n // SUBLANES - 1
    zprev = pl.BlockSpec((None, SUBLANES, W), lambda b, i: (b, jnp.maximum(i * per - 1, 0), 0))
    znext = pl.BlockSpec((None, SUBLANES, W), lambda b, i: (b, jnp.minimum((i + 1) * per, last), 0))
    consts = [sconv_w, w_out, ln_g, ln_b]
    return pl.pallas_call(
        functools.partial(_cd_out_kernel, tm=tm),
        grid=(bsz, n // tm),
        in_specs=[tok(D_MODEL), tok(W), zprev, znext, tok(W), tok(W), _mod_spec(gate)]
        + [_bcast_spec(a, 2) for a in consts],
        out_specs=tok(D_MODEL),
        out_shape=jax.ShapeDtypeStruct((bsz, n, D_MODEL), F32),
        compiler_params=_params("parallel", "arbitrary"),
        name="cd_out_ln",
    )(x, z, z, z, cg, yd, gate, *consts)


def _pair_gate_weights(w_a, w_x):
    def pair_blockdiag(w):
        wp = w.reshape(N_PAIR, 2, LRU_HEAD_DIM, LRU_HEAD_DIM)
        zero = jnp.zeros_like(wp[:, 0])
        top = jnp.concatenate([wp[:, 0], zero], axis=-1)
        bot = jnp.concatenate([zero, wp[:, 1]], axis=-1)
        return jnp.concatenate([top, bot], axis=-2)
    return jnp.concatenate([pair_blockdiag(w_a[0]), pair_blockdiag(w_x[0]),
                            pair_blockdiag(w_a[1]), pair_blockdiag(w_x[1])], axis=-1).astype(BF16)


def _rope_tables(n):
    rows = n // GRID_W
    row = jnp.repeat(jnp.arange(rows, dtype=F32), GRID_W)
    col = jnp.tile(jnp.arange(GRID_W, dtype=F32), rows)
    n_freq = DIFF_HEAD_DIM // 4
    inv = ROPE_BASE ** (-jnp.arange(n_freq, dtype=F32) / n_freq)
    ang_r = row[:, None] * inv
    ang_c = col[:, None] * inv
    ang = jnp.concatenate([ang_r, ang_r, ang_c, ang_c], axis=-1)
    sign = jnp.tile(jnp.concatenate([-jnp.ones((n_freq,), F32), jnp.ones((n_freq,), F32)]), 2)
    reps = W // DIFF_HEAD_DIM
    return jnp.tile(jnp.cos(ang), (1, reps)), jnp.tile(jnp.sin(ang) * sign, (1, reps))


def kernel(x, c, ctx, c_ctx, w_mod, b_mod, ln_g, ln_b, ab_w_in, ab_w_out, lru_conv_w, lru_conv_b, lru_w_a, lru_b_a,
           lru_w_x, lru_b_x, lru_lambda, sgu_ln_g, sgu_ln_b, sgu_w, sgu_b, cd_w_in, cd_w_out, sconv_w, diff_lambda,
           diff_subln_g):
    bsz, n, _ = x.shape
    nc = ctx.shape[1]
    row = lambda a: a.reshape(1, -1)

    n_cond = -(-(bsz + 1) // SUBLANES) * SUBLANES
    cond = jnp.zeros((n_cond, D_MODEL), F32).at[:bsz].set(c).at[bsz].set(c_ctx)
    mod = _modulation(cond, w_mod, b_mod)

    def mods(l):
        lat = [mod[l, :bsz, j * D_MODEL:(j + 1) * D_MODEL][:, None, :] for j in range(3)]
        con = [mod[l, bsz:bsz + 1, j * D_MODEL:(j + 1) * D_MODEL][:, None, :] for j in range(3)]
        return lat, con

    (shift, scale, gate), (shift_c, scale_c, gate_c) = mods(0)
    w_in = ab_w_in[0].astype(BF16)
    w_out = ab_w_out[0].astype(BF16)
    sgu_args = (row(sgu_ln_g[0]), row(sgu_ln_b[0]), sgu_w[0].astype(BF16),
                jnp.broadcast_to(sgu_b[0][:, :, None], (SGU_GROUPS, SGU_CHUNK, LANES)))
    lx, ga, yb = _ab_in(x, shift, scale, w_in, *sgu_args, tm=512)
    lxc, gac, ybc = _ab_in(ctx, shift_c, scale_c, w_in, *sgu_args, tm=nc)
    wpair = _pair_gate_weights(lru_w_a[0], lru_w_x[0])
    ya, yac = _lru(lx, lxc, ga, gac, lru_conv_w[0], row(lru_conv_b[0]), wpair, lru_b_a[0], lru_b_x[0], lru_lambda[0])
    x1 = _ab_out(x, ya, yb, gate, w_out, row(ln_g[0]), row(ln_b[0]), tm=512)
    ctx1 = _ab_out(ctx, yac, ybc, gate_c, w_out, row(ln_g[0]), row(ln_b[0]), tm=nc)

    (shift, scale, gate), (shift_c, scale_c, _) = mods(1)
    w_in = cd_w_in[0].astype(BF16)
    w_out = cd_w_out[0].astype(BF16)
    cos, sin_signed = _rope_tables(n)
    z, cg, q, k, v, ag = _cd_in(x1, shift, scale, w_in, cos, sin_signed, tm=512)
    kc, vc = _ctx_kv(ctx1, shift_c, scale_c, w_in[:, 5 * W:7 * W], tm=nc)
    yd = _attention(q, kc, k, vc, v, ag, diff_lambda[0], row(diff_subln_g[0]), tq=512, ck=256)
    return _cd_out(x1, z, cg, yd, gate, sconv_w[0], w_out, row(ln_g[1]), row(ln_b[1]), tm=512)
```

```python
import functools
import math

import jax
import jax.numpy as jnp
from jax import lax
from jax.experimental import pallas as pl
from jax.experimental.pallas import tpu as pltpu

D_MODEL = 1024
DEPTH = 2
GRID_W = 64
W = D_MODEL // 2
LRU_HEAD_DIM = 64
LRU_C = 8.0
LRU_PAIR = 2 * LRU_HEAD_DIM
N_PAIR = W // LRU_PAIR
SGU_CHUNK = 128
SGU_GROUPS = 4
DIFF_V_DIM = 128
DIFF_HEADS = W // DIFF_V_DIM
DIFF_HEAD_DIM = DIFF_V_DIM // 2
ROPE_BASE = 10000.0
DEEPNORM_ALPHA = (2 * DEPTH) ** 0.25
LN_EPS = 1e-5
LAM_INIT_1 = 0.8 - 0.6 * math.exp(-0.3 * 1)
LOG2_E = math.log2(math.e)

SUBLANES = 8
LANES = 128
VMEM_LIMIT = 56 * 1024 * 1024

F32 = jnp.float32
BF16 = jnp.bfloat16


def _silu(x):
    return x * jax.nn.sigmoid(x)


def _dot(a, b):
    return jnp.dot(a, b, preferred_element_type=F32)


def _params(*sem):
    return pltpu.CompilerParams(dimension_semantics=sem, vmem_limit_bytes=VMEM_LIMIT)


def _bcast_spec(arr, nd_grid):
    zeros = (0,) * arr.ndim
    return pl.BlockSpec(arr.shape, lambda *_: zeros)


def _mod_spec(arr):
    if arr.shape[0] == 1:
        return pl.BlockSpec((None, 1, D_MODEL), lambda b, i: (0, 0, 0))
    return pl.BlockSpec((None, 1, D_MODEL), lambda b, i: (b, 0, 0))


def _mod_kernel(cond_ref, w_ref, b_ref, o_ref):
    s = _silu(cond_ref[...]).astype(BF16)
    o_ref[...] = _dot(s, w_ref[...].astype(BF16)) + b_ref[...]


def _modulation(cond, w_mod, b_mod):
    n = cond.shape[0]
    return pl.pallas_call(
        _mod_kernel,
        grid=(DEPTH, 3),
        in_specs=[
            pl.BlockSpec((n, D_MODEL), lambda l, j: (0, 0)),
            pl.BlockSpec((None, D_MODEL, D_MODEL), lambda l, j: (l, 0, j)),
            pl.BlockSpec((None, 1, D_MODEL), lambda l, j: (l, 0, j)),
        ],
        out_specs=pl.BlockSpec((None, n, D_MODEL), lambda l, j: (l, 0, j)),
        out_shape=jax.ShapeDtypeStruct((DEPTH, n, 3 * D_MODEL), F32),
        compiler_params=_params("arbitrary", "arbitrary"),
        name="modulation",
    )(cond, w_mod, b_mod.reshape(DEPTH, 1, 3 * D_MODEL))


def _ab_in_kernel(x_ref, shift_ref, scale_ref, w_ref, lng_ref, lnb_ref, ws_ref, bs_ref,
                  lx_ref, ga_ref, yb_ref, *, tm):
    h = (x_ref[...] * (1.0 + scale_ref[...]) + shift_ref[...]).astype(BF16)

    def proj(j):
        return _dot(h, w_ref[:, j * W:(j + 1) * W])

    lx_ref[...] = proj(0)
    ga_ref[...] = _silu(proj(1)).astype(BF16)
    u = jax.nn.gelu(proj(2))
    v = jax.nn.gelu(proj(3))
    mu = jnp.mean(v, axis=-1, keepdims=True)
    vc = v - mu
    var = jnp.mean(vc * vc, axis=-1, keepdims=True)
    vn = ((vc * lax.rsqrt(var + LN_EPS)) * lng_ref[...] + lnb_ref[...]).astype(BF16)
    ug = u * _silu(proj(4))
    for c in range(tm // SGU_CHUNK):
        rows = slice(c * SGU_CHUNK, (c + 1) * SGU_CHUNK)
        for g in range(SGU_GROUPS):
            cols = slice(g * LANES, (g + 1) * LANES)
            s = _dot(ws_ref[g], vn[rows, cols]) + bs_ref[g]
            yb_ref[rows, cols] = (ug[rows, cols] * s).astype(BF16)


def _ab_in(x, shift, scale, w_in, sgu_ln_g, sgu_ln_b, sgu_w, sgu_b, tm):
    bsz, n, _ = x.shape
    tok = lambda width: pl.BlockSpec((None, tm, width), lambda b, i: (b, i, 0))
    consts = [w_in, sgu_ln_g, sgu_ln_b, sgu_w, sgu_b]
    return pl.pallas_call(
        functools.partial(_ab_in_kernel, tm=tm),
        grid=(bsz, n // tm),
        in_specs=[tok(D_MODEL), _mod_spec(shift), _mod_spec(scale)] + [_bcast_spec(a, 2) for a in consts],
        out_specs=[tok(W), tok(W), tok(W)],
        out_shape=[jax.ShapeDtypeStruct((bsz, n, W), F32),
                   jax.ShapeDtypeStruct((bsz, n, W), BF16),
                   jax.ShapeDtypeStruct((bsz, n, W), BF16)],
        compiler_params=_params("parallel", "parallel"),
        name="ab_in_sgu",
    )(x, shift, scale, *consts)


SCAN_SEGMENTS = SUBLANES
SCAN_BLOCK = 32


def _lru_kernel(lx_ref, lxc_ref, ga_ref, gac_ref, cw_ref, cb_ref, wp_ref, ba_ref, bx_ref, lam_ref,
                ya_ref, yac_ref, xp_ref, xpc_ref, af_ref, bf_ref, ar_ref, br_ref, hf_ref, hr_ref, *, n, nc, tr):
    nt = n + nc
    lseg = nt // SCAN_SEGMENTS
    pad = SUBLANES
    wl = lx_ref.shape[-1]
    npair = wl // LANES
    zeros = jnp.zeros((pad, wl), F32)
    for ref, src, length in ((xp_ref, lx_ref, n), (xpc_ref, lxc_ref, nc)):
        ref[0:pad, :] = zeros
        ref[pad + length:2 * pad + length, :] = zeros
        ref[pad:pad + length, :] = src[...]

    lam = lam_ref[...]
    neg = -lam
    softplus = jnp.maximum(neg, 0.0) + jnp.log1p(jnp.exp(-jnp.abs(neg)))
    rate = LRU_C * softplus
    rate_log2 = -LOG2_E * rate
    cw = cw_ref[...]
    cb = cb_ref[...]

    def block_rows(pos):
        if isinstance(pos, int):
            seg, step = divmod(pos, lseg)
        else:
            seg = lax.div(pos, lseg)
            step = pos - seg * lseg
        return pl.ds(step * SCAN_SEGMENTS + seg, SCAN_BLOCK, stride=SCAN_SEGMENTS)

    def coeffs(src_ref, r0, f_off, r_off):
        xw = src_ref[pl.ds(r0, tr + 2 * pad), :]
        xc = cb + sum(cw[k:k + 1, :] * xw[pad - 2 + k:pad - 2 + k + tr, :] for k in range(4))
        xcb = xc.astype(BF16)
        dst = [[block_rows(off + r0 + blk) for blk in range(0, tr, SCAN_BLOCK)] for off in (f_off, r_off)]
        for j in range(npair):
            cols = slice(j * LANES, (j + 1) * LANES)
            gts = _dot(xcb[:, cols], wp_ref[j])
            xj = xc[:, cols]
            for d, (a_ref, b_ref) in enumerate(((af_ref, bf_ref), (ar_ref, br_ref))):
                gate_r = jax.nn.sigmoid(gts[:, (2 * d) * LANES:(2 * d + 1) * LANES] + ba_ref[d:d + 1, cols])
                gate_i = jax.nn.sigmoid(gts[:, (2 * d + 1) * LANES:(2 * d + 2) * LANES] + bx_ref[d:d + 1, cols])
                a = jnp.exp2(gate_r * rate_log2[d:d + 1, cols])
                one_minus_a2 = jnp.tanh(gate_r * rate[d:d + 1, cols]) * (1.0 + a * a)
                root = jnp.where(one_minus_a2 > 0.0, one_minus_a2 * lax.rsqrt(one_minus_a2), 0.0)
                b = root * (gate_i * xj)
                for rows, blk in zip(dst[d], range(0, tr, SCAN_BLOCK)):
                    a_ref[j, rows, :] = a[blk:blk + SCAN_BLOCK, :]
                    b_ref[j, rows, :] = b[blk:blk + SCAN_BLOCK, :]

    def lat_tile(i, carry):
        coeffs(xp_ref, pl.multiple_of(i * tr, tr), nc, 0)
        return carry

    lax.fori_loop(0, n // tr, lat_tile, 0)
    for i in range(nc // tr):
        coeffs(xpc_ref, i * tr, 0, n)

    chains = [(a_ref, b_ref, h_ref, j, rev)
              for a_ref, b_ref, h_ref, rev in ((af_ref, bf_ref, hf_ref, False), (ar_ref, br_ref, hr_ref, True))
              for j in range(npair)]

    def step_rows(t, rev):
        return pl.ds(pl.multiple_of((lseg - 1 - t if rev else t) * SCAN_SEGMENTS, SCAN_SEGMENTS), SCAN_SEGMENTS)

    def totals_step(t, carry):
        out = []
        for (a_ref, b_ref, _, j, rev), (decay, h) in zip(chains, carry):
            rows = step_rows(t, rev)
            a = a_ref[j, rows, :]
            out.append((a * decay, a * h + b_ref[j, rows, :]))
        return tuple(out)

    one = jnp.ones((SCAN_SEGMENTS, LANES), F32)
    zero = jnp.zeros((SCAN_SEGMENTS, LANES), F32)
    totals = lax.fori_loop(0, lseg, totals_step, tuple((one, zero) for _ in chains), unroll=8)

    def entering_state(decay, h_end, rev):
        order = range(SCAN_SEGMENTS - 1, -1, -1) if rev else range(SCAN_SEGMENTS)
        rows = [None] * SCAN_SEGMENTS
        state = jnp.zeros((1, LANES), F32)
        for s in order:
            rows[s] = state
            state = decay[s:s + 1, :] * state + h_end[s:s + 1, :]
        return jnp.concatenate(rows, axis=0)

    starts = tuple(entering_state(decay, h_end, chain[-1]) for chain, (decay, h_end) in zip(chains, totals))

    def states_step(t, carry):
        out = []
        for (a_ref, b_ref, h_ref, j, rev), h in zip(chains, carry):
            rows = step_rows(t, rev)
            h = a_ref[j, rows, :] * h + b_ref[j, rows, :]
            h_ref[j, rows, :] = h
            out.append(h)
        return tuple(out)

    lax.fori_loop(0, lseg, states_step, starts, unroll=8)

    def write_out(y_ref, g_ref, r0, f_pos, r_pos, rows):
        for blk in range(0, rows, SCAN_BLOCK):
            dst = pl.ds(r0 + blk, SCAN_BLOCK)
            rows_f = block_rows(f_pos + blk)
            rows_r = block_rows(r_pos + blk)
            for j in range(npair):
                cols = slice(j * LANES, (j + 1) * LANES)
                h = hf_ref[j, rows_f, :] + hr_ref[j, rows_r, :]
                y_ref[dst, cols] = (h * g_ref[dst, cols].astype(F32)).astype(BF16)

    def out_tile(i, carry):
        r0 = pl.multiple_of(i * tr, tr)
        write_out(ya_ref, ga_ref, r0, nc + r0, r0, tr)
        return carry

    lax.fori_loop(0, n // tr, out_tile, 0)
    write_out(yac_ref, gac_ref, 0, 0, n, nc)


def _lru(lx, lxc, ga, gac, conv_w, conv_b, wpair, b_a, b_x, lam, tr=256):
    bsz, n, _ = lx.shape
    nc = lxc.shape[1]
    lseg, rem = divmod(n + nc, SCAN_SEGMENTS)
    assert rem == 0 and lseg % SCAN_BLOCK == 0 and n % tr == 0 and nc % tr == 0 and tr % SCAN_BLOCK == 0
    npair = 2
    wl = npair * LANES
    seq = lambda length: pl.BlockSpec((None, length, wl), lambda b, g: (b, 0, g))
    vec = lambda rows: pl.BlockSpec((rows, wl), lambda b, g: (0, g))
    pad = 2 * SUBLANES
    coeff = pltpu.VMEM((npair, SCAN_SEGMENTS * lseg, LANES), F32)
    return pl.pallas_call(
        functools.partial(_lru_kernel, n=n, nc=nc, tr=tr),
        grid=(bsz, N_PAIR // npair),
        in_specs=[seq(n), seq(nc), seq(n), seq(nc), vec(conv_w.shape[0]), vec(1),
                  pl.BlockSpec((npair,) + wpair.shape[1:], lambda b, g: (g, 0, 0)), vec(2), vec(2), vec(2)],
        out_specs=[seq(n), seq(nc)],
        out_shape=[jax.ShapeDtypeStruct((bsz, n, W), BF16), jax.ShapeDtypeStruct((bsz, nc, W), BF16)],
        scratch_shapes=[pltpu.VMEM((n + pad, wl), F32), pltpu.VMEM((nc + pad, wl), F32)] + [coeff] * 6,
        compiler_params=_params("parallel", "parallel"),
        name="rglru",
    )(lx, lxc, ga, gac, conv_w, conv_b, wpair, b_a, b_x, lam)


def _residual_ln(x, y, gate, g, b):
    z = DEEPNORM_ALPHA * x + gate * y
    mu = jnp.mean(z, axis=-1, keepdims=True)
    zc = z - mu
    var = jnp.mean(zc * zc, axis=-1, keepdims=True)
    return (zc * lax.rsqrt(var + LN_EPS)) * g + b


def _ctx_out_kv_kernel(x_ref, ya_ref, yb_ref, gate_ref, wo_ref, g_ref, b_ref, shift_ref, scale_ref, w_ref,
                       k_ref, v_ref):
    y = _dot(ya_ref[...], wo_ref[0:W, :]) + _dot(yb_ref[...], wo_ref[W:2 * W, :])
    c1 = _residual_ln(x_ref[...], y, gate_ref[...], g_ref[...], b_ref[...])
    h = (c1 * (1.0 + scale_ref[...]) + shift_ref[...]).astype(BF16)
    k_ref[...] = _dot(h, w_ref[:, 0:W]).astype(BF16)
    v_ref[...] = _dot(h, w_ref[:, W:2 * W]).astype(BF16)


def _ctx_out_kv(ctx, ya, yb, gate0, w_out, ln_g, ln_b, shift, scale, w_kv, tm):
    bsz, n, _ = ctx.shape
    tok = lambda width: pl.BlockSpec((None, tm, width), lambda b, i: (b, i, 0))
    return pl.pallas_call(
        _ctx_out_kv_kernel,
        grid=(bsz, n // tm),
        in_specs=[tok(D_MODEL), tok(W), tok(W), _mod_spec(gate0), _bcast_spec(w_out, 2), _bcast_spec(ln_g, 2),
                  _bcast_spec(ln_b, 2), _mod_spec(shift), _mod_spec(scale), _bcast_spec(w_kv, 2)],
        out_specs=[tok(W), tok(W)],
        out_shape=[jax.ShapeDtypeStruct((bsz, n, W), BF16)] * 2,
        compiler_params=_params("parallel", "parallel"),
        name="ctx_out_kv",
    )(ctx, ya, yb, gate0, w_out, ln_g, ln_b, shift, scale, w_kv)


def _rope(t, cos, sin_signed, first_half):
    parts = []
    for j in range(W // LANES):
        tj = t[:, j * LANES:(j + 1) * LANES]
        parts.append(jnp.where(first_half, pltpu.roll(tj, LANES - 16, 1), pltpu.roll(tj, 16, 1)))
    return t * cos + jnp.concatenate(parts, axis=1) * sin_signed


def _ab_out_cd_in_kernel(x_ref, ya_ref, yb_ref, gate_ref, wo_ref, g_ref, b_ref, shift_ref, scale_ref, w_ref,
                         cos_ref, sin_ref, x1_ref, z_ref, cg_ref, q_ref, k_ref, v_ref, ag_ref, *, tm):
    y = _dot(ya_ref[...], wo_ref[0:W, :]) + _dot(yb_ref[...], wo_ref[W:2 * W, :])
    x1 = _residual_ln(x_ref[...], y, gate_ref[...], g_ref[...], b_ref[...])
    x1_ref[...] = x1
    h = (x1 * (1.0 + scale_ref[...]) + shift_ref[...]).astype(BF16)

    def proj(j):
        return _dot(h, w_ref[:, j * W:(j + 1) * W])

    z_ref[...] = proj(2) * proj(0)
    cg_ref[...] = (proj(1) * _silu(proj(3))).astype(BF16)
    cos = cos_ref[...]
    sin = sin_ref[...]
    lane = lax.broadcasted_iota(jnp.int32, (tm, LANES), 1)
    first_half = (lane % 32) < 16
    q_ref[...] = (_rope(proj(4), cos, sin, first_half) * (DIFF_HEAD_DIM ** -0.5 * LOG2_E)).astype(BF16)
    k_ref[...] = _rope(proj(5), cos, sin, first_half).astype(BF16)
    v_ref[...] = proj(6).astype(BF16)
    ag_ref[...] = _silu(proj(7)).astype(BF16)


def _ab_out_cd_in(x, ya, yb, gate0, w_out, ln_g, ln_b, shift, scale, w_in, cos, sin_signed, tm):
    bsz, n, _ = x.shape
    tok = lambda width: pl.BlockSpec((None, tm, width), lambda i, b: (b, i, 0))
    mod = pl.BlockSpec((None, 1, D_MODEL), lambda i, b: (b, 0, 0))
    table = pl.BlockSpec((tm, W), lambda i, b: (i, 0))
    const = lambda a: pl.BlockSpec(a.shape, lambda i, b: (0,) * a.ndim, pipeline_mode=pl.Buffered(1))
    return pl.pallas_call(
        functools.partial(_ab_out_cd_in_kernel, tm=tm),
        grid=(n // tm, bsz),
        in_specs=[tok(D_MODEL), tok(W), tok(W), mod, const(w_out), const(ln_g), const(ln_b), mod, mod, const(w_in),
                  table, table],
        out_specs=[tok(D_MODEL)] + [tok(W)] * 6,
        out_shape=[jax.ShapeDtypeStruct((bsz, n, D_MODEL), F32), jax.ShapeDtypeStruct((bsz, n, W), F32)]
        + [jax.ShapeDtypeStruct((bsz, n, W), BF16)] * 5,
        compiler_params=_params("parallel", "parallel"),
        name="ab_out_cd_in",
    )(x, ya, yb, gate0, w_out, ln_g, ln_b, shift, scale, w_in, cos, sin_signed)


def _attn_kernel(q_ref, kc_ref, k_ref, vc_ref, v_ref, ag_ref, dl_ref, g_ref, o_ref,
                 kall_ref, vall_ref, s_ref, p_ref, al_ref, m_ref, acc_ref, *, tq, ck):
    nc = kc_ref.shape[0]
    n = k_ref.shape[0]
    nck = (n + nc) // ck

    @pl.when(pl.program_id(2) == 0)
    def _():
        kall_ref[0:n, :] = k_ref[...]
        kall_ref[n:n + nc, :] = kc_ref[...]
        vall_ref[0:n, 0:DIFF_V_DIM] = v_ref[...]
        vall_ref[n:n + nc, 0:DIFF_V_DIM] = vc_ref[...]
        vall_ref[:, DIFF_V_DIM:2 * DIFF_V_DIM] = jnp.ones((n + nc, DIFF_V_DIM), BF16)

    dl = dl_ref[...]
    lam = (jnp.exp(jnp.sum(dl[0:1, :] * dl[1:2, :], axis=-1, keepdims=True))
           - jnp.exp(jnp.sum(dl[2:3, :] * dl[3:4, :], axis=-1, keepdims=True)) + LAM_INIT_1)
    q = q_ref[...]
    lane = lax.broadcasted_iota(jnp.int32, (tq, DIFF_V_DIM), 1)
    zero = jnp.zeros_like(q)
    qq = jnp.concatenate([jnp.where(lane < DIFF_HEAD_DIM, q, zero), jnp.where(lane >= DIFF_HEAD_DIM, q, zero)], axis=0)
    nt = (((1,), (1,)), ((), ()))

    def scores(j, slot):
        kj = kall_ref[pl.ds(pl.multiple_of(j * ck, ck), ck), :]
        s_ref[slot] = lax.dot_general(qq, kj, nt, preferred_element_type=F32)

    def lanes(t):
        return jnp.concatenate([t] * (ck // LANES), axis=1)

    def softmax(slot, first):
        s = s_ref[slot]
        sm = s[:, 0:LANES]
        for c in range(1, ck // LANES):
            sm = jnp.maximum(sm, s[:, c * LANES:(c + 1) * LANES])
        m_new = jnp.broadcast_to(jnp.max(sm, axis=-1, keepdims=True), (2 * tq, LANES))
        if not first:
            m_old = m_ref[...]
            m_new = jnp.maximum(m_old, m_new)
            al_ref[slot] = jnp.exp2(m_old - m_new)
        m_ref[...] = m_new
        p_ref[slot] = jnp.exp2(s - lanes(m_new)).astype(BF16)

    def weighted_values(j, slot, first):
        vj = vall_ref[pl.ds(pl.multiple_of(j * ck, ck), ck), :]
        pv = _dot(p_ref[slot], vj)
        if first:
            acc_ref[...] = pv
        else:
            alpha = al_ref[slot]
            acc_ref[...] = jnp.concatenate([alpha, alpha], axis=1) * acc_ref[...] + pv

    slots = s_ref.shape[0]
    for t in range(nck + 2):
        if t < nck:
            scores(t, t % slots)
        if 1 <= t <= nck:
            softmax((t - 1) % slots, t == 1)
        if t >= 2:
            weighted_values(t - 2, (t - 2) % slots, t == 2)

    acc = acc_ref[:, 0:DIFF_V_DIM]
    w = 1.0 / acc_ref[:, DIFF_V_DIM:2 * DIFF_V_DIM]
    o = acc[0:tq, :] * w[0:tq, :] - acc[tq:2 * tq, :] * (lam * w[tq:2 * tq, :])
    o = (o * lax.rsqrt(jnp.mean(o * o, axis=-1, keepdims=True) + LN_EPS)) * g_ref[...] * (1.0 - LAM_INIT_1)
    o_ref[...] = (o * ag_ref[...].astype(F32)).astype(BF16)


def _attention(q, kc, k, vc, v, ag, diff_lambda, subln_g, tq, ck):
    bsz, n, _ = q.shape
    nc = kc.shape[1]
    qblk = pl.BlockSpec((None, tq, DIFF_V_DIM), lambda b, h, i: (b, i, h))
    keys = lambda length: pl.BlockSpec((None, length, DIFF_V_DIM), lambda b, h, i: (b, 0, h))
    consts = [diff_lambda, subln_g]
    slots = 2
    return pl.pallas_call(
        functools.partial(_attn_kernel, tq=tq, ck=ck),
        grid=(bsz, DIFF_HEADS, n // tq),
        in_specs=[qblk, keys(nc), keys(n), keys(nc), keys(n), qblk] + [_bcast_spec(a, 3) for a in consts],
        out_specs=qblk,
        out_shape=jax.ShapeDtypeStruct((bsz, n, W), BF16),
        scratch_shapes=[pltpu.VMEM((n + nc, DIFF_V_DIM), BF16), pltpu.VMEM((n + nc, 2 * DIFF_V_DIM), BF16),
                        pltpu.VMEM((slots, 2 * tq, ck), F32), pltpu.VMEM((slots, 2 * tq, ck), BF16),
                        pltpu.VMEM((slots, 2 * tq, LANES), F32), pltpu.VMEM((2 * tq, LANES), F32),
                        pltpu.VMEM((2 * tq, 2 * DIFF_V_DIM), F32)],
        compiler_params=_params("parallel", "parallel", "arbitrary"),
        name="diff_attention",
    )(q, kc, k, vc, v, ag, *consts)


def _cd_out_kernel(x_ref, z_ref, zprev_ref, znext_ref, cg_ref, yd_ref, gate_ref, cw_ref, w_ref, g_ref, b_ref,
                   o_ref, *, tm):
    i = pl.program_id(1)
    z = z_ref[...]
    row = lax.broadcasted_iota(jnp.int32, (tm, W), 0)
    prev_row = jnp.where(i > 0, zprev_ref[SUBLANES - 1:SUBLANES, :], 0.0)
    next_row = jnp.where(i < pl.num_programs(1) - 1, znext_ref[0:1, :], 0.0)
    z_m1 = jnp.where(row == 0, prev_row, pltpu.roll(z, 1, 0))
    z_p1 = jnp.where(row == tm - 1, next_row, pltpu.roll(z, tm - 1, 0))
    cw = cw_ref[...]
    conv = cw[0:1, :] * z_m1 + cw[1:2, :] * z + cw[2:3, :] * z_p1
    yc = (cg_ref[...].astype(F32) * conv).astype(BF16)
    y = _dot(yc, w_ref[0:W, :]) + _dot(yd_ref[...], w_ref[W:2 * W, :])
    o_ref[...] = _residual_ln(x_ref[...], y, gate_ref[...], g_ref[...], b_ref[...])


def _cd_out(x, z, cg, yd, gate, sconv_w, w_out, ln_g, ln_b, tm):
    bsz, n, _ = x.shape
    tok = lambda width: pl.BlockSpec((None, tm, width), lambda b, i: (b, i, 0))
    per = tm // SUBLANES
    last = n // SUBLANES - 1
    zprev = pl.BlockSpec((None, SUBLANES, W), lambda b, i: (b, jnp.maximum(i * per - 1, 0), 0))
    znext = pl.BlockSpec((None, SUBLANES, W), lambda b, i: (b, jnp.minimum((i + 1) * per, last), 0))
    consts = [sconv_w, w_out, ln_g, ln_b]
    return pl.pallas_call(
        functools.partial(_cd_out_kernel, tm=tm),
        grid=(bsz, n // tm),
        in_specs=[tok(D_MODEL), tok(W), zprev, znext, tok(W), tok(W), _mod_spec(gate)]
        + [_bcast_spec(a, 2) for a in consts],
        out_specs=tok(D_MODEL),
        out_shape=jax.ShapeDtypeStruct((bsz, n, D_MODEL), F32),
        compiler_params=_params("parallel", "arbitrary"),
        name="cd_out_ln",
    )(x, z, z, z, cg, yd, gate, *consts)


def _pair_gate_weights(w_a, w_x):
    def pair_blockdiag(w):
        wp = w.reshape(N_PAIR, 2, LRU_HEAD_DIM, LRU_HEAD_DIM)
        zero = jnp.zeros_like(wp[:, 0])
        top = jnp.concatenate([wp[:, 0], zero], axis=-1)
        bot = jnp.concatenate([zero, wp[:, 1]], axis=-1)
        return jnp.concatenate([top, bot], axis=-2)
    return jnp.concatenate([pair_blockdiag(w_a[0]), pair_blockdiag(w_x[0]),
                            pair_blockdiag(w_a[1]), pair_blockdiag(w_x[1])], axis=-1).astype(BF16)


def _rope_tables(n):
    rows = n // GRID_W
    row = jnp.repeat(jnp.arange(rows, dtype=F32), GRID_W)
    col = jnp.tile(jnp.arange(GRID_W, dtype=F32), rows)
    n_freq = DIFF_HEAD_DIM // 4
    inv = ROPE_BASE ** (-jnp.arange(n_freq, dtype=F32) / n_freq)
    ang_r = row[:, None] * inv
    ang_c = col[:, None] * inv
    ang = jnp.concatenate([ang_r, ang_r, ang_c, ang_c], axis=-1)
    sign = jnp.tile(jnp.concatenate([-jnp.ones((n_freq,), F32), jnp.ones((n_freq,), F32)]), 2)
    reps = W // DIFF_HEAD_DIM
    return jnp.tile(jnp.cos(ang), (1, reps)), jnp.tile(jnp.sin(ang) * sign, (1, reps))


def kernel(x, c, ctx, c_ctx, w_mod, b_mod, ln_g, ln_b, ab_w_in, ab_w_out, lru_conv_w, lru_conv_b, lru_w_a, lru_b_a,
           lru_w_x, lru_b_x, lru_lambda, sgu_ln_g, sgu_ln_b, sgu_w, sgu_b, cd_w_in, cd_w_out, sconv_w, diff_lambda,
           diff_subln_g):
    bsz, n, _ = x.shape
    nc = ctx.shape[1]
    row = lambda a: a.reshape(1, -1)

    n_cond = -(-(bsz + 1) // SUBLANES) * SUBLANES
    cond = jnp.zeros((n_cond, D_MODEL), F32).at[:bsz].set(c).at[bsz].set(c_ctx)
    mod = _modulation(cond, w_mod, b_mod)

    def mods(l):
        lat = [mod[l, :bsz, j * D_MODEL:(j + 1) * D_MODEL][:, None, :] for j in range(3)]
        con = [mod[l, bsz:bsz + 1, j * D_MODEL:(j + 1) * D_MODEL][:, None, :] for j in range(3)]
        return lat, con

    (shift, scale, gate), (shift_c, scale_c, gate_c) = mods(0)
    w_in = ab_w_in[0].astype(BF16)
    w_out = ab_w_out[0].astype(BF16)
    sgu_args = (row(sgu_ln_g[0]), row(sgu_ln_b[0]), sgu_w[0].astype(BF16),
                jnp.broadcast_to(sgu_b[0][:, :, None], (SGU_GROUPS, SGU_CHUNK, LANES)))
    lx, ga, yb = _ab_in(x, shift, scale, w_in, *sgu_args, tm=512)
    lxc, gac, ybc = _ab_in(ctx, shift_c, scale_c, w_in, *sgu_args, tm=nc)
    wpair = _pair_gate_weights(lru_w_a[0], lru_w_x[0])
    ya, yac = _lru(lx, lxc, ga, gac, lru_conv_w[0], row(lru_conv_b[0]), wpair, lru_b_a[0], lru_b_x[0], lru_lambda[0])

    (shift, scale, gate1), (shift_c, scale_c, _) = mods(1)
    w_in = cd_w_in[0].astype(BF16)
    cos, sin_signed = _rope_tables(n)
    x1, z, cg, q, k, v, ag = _ab_out_cd_in(x, ya, yb, gate, w_out, row(ln_g[0]), row(ln_b[0]), shift, scale, w_in,
                                           cos, sin_signed, tm=512)
    kc, vc = _ctx_out_kv(ctx, yac, ybc, gate_c, w_out, row(ln_g[0]), row(ln_b[0]), shift_c, scale_c,
                         w_in[:, 5 * W:7 * W], tm=nc)
    gate = gate1
    w_out = cd_w_out[0].astype(BF16)
    yd = _attention(q, kc, k, vc, v, ag, diff_lambda[0], row(diff_subln_g[0]), tq=512, ck=768)
    return _cd_out(x1, z, cg, yd, gate, sconv_w[0], w_out, row(ln_g[1]), row(ln_b[1]), tm=512)
```

```python
import functools
import math

import jax
import jax.numpy as jnp
from jax import lax
from jax.experimental import pallas as pl
from jax.experimental.pallas import tpu as pltpu

D_MODEL = 1024
DEPTH = 2
GRID_W = 64
W = D_MODEL // 2
LRU_HEAD_DIM = 64
LRU_C = 8.0
LRU_PAIR = 2 * LRU_HEAD_DIM
N_PAIR = W // LRU_PAIR
SGU_CHUNK = 128
SGU_GROUPS = 4
DIFF_V_DIM = 128
DIFF_HEADS = W // DIFF_V_DIM
DIFF_HEAD_DIM = DIFF_V_DIM // 2
ROPE_BASE = 10000.0
DEEPNORM_ALPHA = (2 * DEPTH) ** 0.25
LN_EPS = 1e-5
LAM_INIT_1 = 0.8 - 0.6 * math.exp(-0.3 * 1)
LOG2_E = math.log2(math.e)

SUBLANES = 8
LANES = 128
VMEM_LIMIT = 56 * 1024 * 1024

F32 = jnp.float32
BF16 = jnp.bfloat16


def _silu(x):
    return x * jax.nn.sigmoid(x)


def _dot(a, b):
    return jnp.dot(a, b, preferred_element_type=F32)


def _params(*sem):
    return pltpu.CompilerParams(dimension_semantics=sem, vmem_limit_bytes=VMEM_LIMIT)


def _bcast_spec(arr, nd_grid):
    zeros = (0,) * arr.ndim
    return pl.BlockSpec(arr.shape, lambda *_: zeros)


def _mod_spec(arr):
    if arr.shape[0] == 1:
        return pl.BlockSpec((None, 1, D_MODEL), lambda b, i: (0, 0, 0))
    return pl.BlockSpec((None, 1, D_MODEL), lambda b, i: (b, 0, 0))


def _mod_kernel(cond_ref, w_ref, b_ref, o_ref):
    s = _silu(cond_ref[...]).astype(BF16)
    o_ref[...] = _dot(s, w_ref[...].astype(BF16)) + b_ref[...]


def _modulation(cond, w_mod, b_mod):
    n = cond.shape[0]
    return pl.pallas_call(
        _mod_kernel,
        grid=(DEPTH, 3),
        in_specs=[
            pl.BlockSpec((n, D_MODEL), lambda l, j: (0, 0)),
            pl.BlockSpec((None, D_MODEL, D_MODEL), lambda l, j: (l, 0, j)),
            pl.BlockSpec((None, 1, D_MODEL), lambda l, j: (l, 0, j)),
        ],
        out_specs=pl.BlockSpec((None, n, D_MODEL), lambda l, j: (l, 0, j)),
        out_shape=jax.ShapeDtypeStruct((DEPTH, n, 3 * D_MODEL), F32),
        compiler_params=_params("arbitrary", "arbitrary"),
        name="modulation",
    )(cond, w_mod, b_mod.reshape(DEPTH, 1, 3 * D_MODEL))


def _ab_in_kernel(x_ref, shift_ref, scale_ref, w_ref, lng_ref, lnb_ref, ws_ref, bs_ref,
                  lx_ref, ga_ref, yb_ref, *, tm):
    h = (x_ref[...] * (1.0 + scale_ref[...]) + shift_ref[...]).astype(BF16)

    def proj(j):
        return _dot(h, w_ref[:, j * W:(j + 1) * W])

    lx_ref[...] = proj(0)
    ga_ref[...] = _silu(proj(1)).astype(BF16)
    u = jax.nn.gelu(proj(2))
    v = jax.nn.gelu(proj(3))
    mu = jnp.mean(v, axis=-1, keepdims=True)
    vc = v - mu
    var = jnp.mean(vc * vc, axis=-1, keepdims=True)
    vn = ((vc * lax.rsqrt(var + LN_EPS)) * lng_ref[...] + lnb_ref[...]).astype(BF16)
    ug = u * _silu(proj(4))
    for c in range(tm // SGU_CHUNK):
        rows = slice(c * SGU_CHUNK, (c + 1) * SGU_CHUNK)
        for g in range(SGU_GROUPS):
            cols = slice(g * LANES, (g + 1) * LANES)
            s = _dot(ws_ref[g], vn[rows, cols]) + bs_ref[g]
            yb_ref[rows, cols] = (ug[rows, cols] * s).astype(BF16)


def _ab_in(x, shift, scale, w_in, sgu_ln_g, sgu_ln_b, sgu_w, sgu_b, tm):
    bsz, n, _ = x.shape
    tok = lambda width: pl.BlockSpec((None, tm, width), lambda b, i: (b, i, 0))
    consts = [w_in, sgu_ln_g, sgu_ln_b, sgu_w, sgu_b]
    return pl.pallas_call(
        functools.partial(_ab_in_kernel, tm=tm),
        grid=(bsz, n // tm),
        in_specs=[tok(D_MODEL), _mod_spec(shift), _mod_spec(scale)] + [_bcast_spec(a, 2) for a in consts],
        out_specs=[tok(W), tok(W), tok(W)],
        out_shape=[jax.ShapeDtypeStruct((bsz, n, W), F32),
                   jax.ShapeDtypeStruct((bsz, n, W), BF16),
                   jax.ShapeDtypeStruct((bsz, n, W), BF16)],
        compiler_params=_params("parallel", "parallel"),
        name="ab_in_sgu",
    )(x, shift, scale, *consts)


SCAN_SEGMENTS = SUBLANES
SCAN_BLOCK = 32


def _lru_kernel(lx_ref, lxc_ref, ga_ref, gac_ref, cw_ref, cb_ref, wp_ref, ba_ref, bx_ref, lam_ref,
                ya_ref, yac_ref, xp_ref, xpc_ref, af_ref, bf_ref, ar_ref, br_ref, hf_ref, hr_ref, *, n, nc, tr):
    nt = n + nc
    lseg = nt // SCAN_SEGMENTS
    pad = SUBLANES
    wl = lx_ref.shape[-1]
    npair = wl // LANES
    zeros = jnp.zeros((pad, wl), F32)
    for ref, src, length in ((xp_ref, lx_ref, n), (xpc_ref, lxc_ref, nc)):
        ref[0:pad, :] = zeros
        ref[pad + length:2 * pad + length, :] = zeros
        ref[pad:pad + length, :] = src[...]

    lam = lam_ref[...]
    neg = -lam
    softplus = jnp.maximum(neg, 0.0) + jnp.log1p(jnp.exp(-jnp.abs(neg)))
    rate = LRU_C * softplus
    rate_log2 = -LOG2_E * rate
    cw = cw_ref[...]
    cb = cb_ref[...]

    def block_rows(pos):
        if isinstance(pos, int):
            seg, step = divmod(pos, lseg)
        else:
            seg = lax.div(pos, lseg)
            step = pos - seg * lseg
        return pl.ds(step * SCAN_SEGMENTS + seg, SCAN_BLOCK, stride=SCAN_SEGMENTS)

    def coeffs(src_ref, r0, f_off, r_off):
        xw = src_ref[pl.ds(r0, tr + 2 * pad), :]
        xc = cb + sum(cw[k:k + 1, :] * xw[pad - 2 + k:pad - 2 + k + tr, :] for k in range(4))
        xcb = xc.astype(BF16)
        dst = [[block_rows(off + r0 + blk) for blk in range(0, tr, SCAN_BLOCK)] for off in (f_off, r_off)]
        for j in range(npair):
            cols = slice(j * LANES, (j + 1) * LANES)
            gts = _dot(xcb[:, cols], wp_ref[j])
            xj = xc[:, cols]
            for d, (a_ref, b_ref) in enumerate(((af_ref, bf_ref), (ar_ref, br_ref))):
                gate_r = jax.nn.sigmoid(gts[:, (2 * d) * LANES:(2 * d + 1) * LANES] + ba_ref[d:d + 1, cols])
                gate_i = jax.nn.sigmoid(gts[:, (2 * d + 1) * LANES:(2 * d + 2) * LANES] + bx_ref[d:d + 1, cols])
                a = jnp.exp2(gate_r * rate_log2[d:d + 1, cols])
                one_minus_a2 = jnp.tanh(gate_r * rate[d:d + 1, cols]) * (1.0 + a * a)
                root = jnp.where(one_minus_a2 > 0.0, one_minus_a2 * lax.rsqrt(one_minus_a2), 0.0)
                b = root * (gate_i * xj)
                for rows, blk in zip(dst[d], range(0, tr, SCAN_BLOCK)):
                    a_ref[j, rows, :] = a[blk:blk + SCAN_BLOCK, :]
                    b_ref[j, rows, :] = b[blk:blk + SCAN_BLOCK, :]

    def lat_tile(i, carry):
        coeffs(xp_ref, pl.multiple_of(i * tr, tr), nc, 0)
        return carry

    lax.fori_loop(0, n // tr, lat_tile, 0)
    for i in range(nc // tr):
        coeffs(xpc_ref, i * tr, 0, n)

    chains = [(a_ref, b_ref, h_ref, j, rev)
              for a_ref, b_ref, h_ref, rev in ((af_ref, bf_ref, hf_ref, False), (ar_ref, br_ref, hr_ref, True))
              for j in range(npair)]

    def step_rows(t, rev):
        return pl.ds(pl.multiple_of((lseg - 1 - t if rev else t) * SCAN_SEGMENTS, SCAN_SEGMENTS), SCAN_SEGMENTS)

    def totals_step(t, carry):
        out = []
        for (a_ref, b_ref, _, j, rev), (decay, h) in zip(chains, carry):
            rows = step_rows(t, rev)
            a = a_ref[j, rows, :]
            out.append((a * decay, a * h + b_ref[j, rows, :]))
        return tuple(out)

    one = jnp.ones((SCAN_SEGMENTS, LANES), F32)
    zero = jnp.zeros((SCAN_SEGMENTS, LANES), F32)
    totals = lax.fori_loop(0, lseg, totals_step, tuple((one, zero) for _ in chains), unroll=8)

    def entering_state(decay, h_end, rev):
        order = range(SCAN_SEGMENTS - 1, -1, -1) if rev else range(SCAN_SEGMENTS)
        rows = [None] * SCAN_SEGMENTS
        state = jnp.zeros((1, LANES), F32)
        for s in order:
            rows[s] = state
            state = decay[s:s + 1, :] * state + h_end[s:s + 1, :]
        return jnp.concatenate(rows, axis=0)

    starts = tuple(entering_state(decay, h_end, chain[-1]) for chain, (decay, h_end) in zip(chains, totals))

    def states_step(t, carry):
        out = []
        for (a_ref, b_ref, h_ref, j, rev), h in zip(chains, carry):
            rows = step_rows(t, rev)
            h = a_ref[j, rows, :] * h + b_ref[j, rows, :]
            h_ref[j, rows, :] = h
            out.append(h)
        return tuple(out)

    lax.fori_loop(0, lseg, states_step, starts, unroll=8)

    def write_out(y_ref, g_ref, r0, f_pos, r_pos, rows):
        for blk in range(0, rows, SCAN_BLOCK):
            dst = pl.ds(r0 + blk, SCAN_BLOCK)
            rows_f = block_rows(f_pos + blk)
            rows_r = block_rows(r_pos + blk)
            for j in range(npair):
                cols = slice(j * LANES, (j + 1) * LANES)
                h = hf_ref[j, rows_f, :] + hr_ref[j, rows_r, :]
                y_ref[dst, cols] = (h * g_ref[dst, cols].astype(F32)).astype(BF16)

    def out_tile(i, carry):
        r0 = pl.multiple_of(i * tr, tr)
        write_out(ya_ref, ga_ref, r0, nc + r0, r0, tr)
        return carry

    lax.fori_loop(0, n // tr, out_tile, 0)
    write_out(yac_ref, gac_ref, 0, 0, n, nc)


def _lru(lx, lxc, ga, gac, conv_w, conv_b, wpair, b_a, b_x, lam, tr=256):
    bsz, n, _ = lx.shape
    nc = lxc.shape[1]
    lseg, rem = divmod(n + nc, SCAN_SEGMENTS)
    assert rem == 0 and lseg % SCAN_BLOCK == 0 and n % tr == 0 and nc % tr == 0 and tr % SCAN_BLOCK == 0
    npair = 2
    wl = npair * LANES
    seq = lambda length: pl.BlockSpec((None, length, wl), lambda b, g: (b, 0, g))
    vec = lambda rows: pl.BlockSpec((rows, wl), lambda b, g: (0, g))
    pad = 2 * SUBLANES
    coeff = pltpu.VMEM((npair, SCAN_SEGMENTS * lseg, LANES), F32)
    return pl.pallas_call(
        functools.partial(_lru_kernel, n=n, nc=nc, tr=tr),
        grid=(bsz, N_PAIR // npair),
        in_specs=[seq(n), seq(nc), seq(n), seq(nc), vec(conv_w.shape[0]), vec(1),
                  pl.BlockSpec((npair,) + wpair.shape[1:], lambda b, g: (g, 0, 0)), vec(2), vec(2), vec(2)],
        out_specs=[seq(n), seq(nc)],
        out_shape=[jax.ShapeDtypeStruct((bsz, n, W), BF16), jax.ShapeDtypeStruct((bsz, nc, W), BF16)],
        scratch_shapes=[pltpu.VMEM((n + pad, wl), F32), pltpu.VMEM((nc + pad, wl), F32)] + [coeff] * 6,
        compiler_params=_params("parallel", "parallel"),
        name="rglru",
    )(lx, lxc, ga, gac, conv_w, conv_b, wpair, b_a, b_x, lam)


def _residual_ln(x, y, gate, g, b):
    z = DEEPNORM_ALPHA * x + gate * y
    mu = jnp.mean(z, axis=-1, keepdims=True)
    zc = z - mu
    var = jnp.mean(zc * zc, axis=-1, keepdims=True)
    return (zc * lax.rsqrt(var + LN_EPS)) * g + b


def _ctx_out_kv_kernel(x_ref, ya_ref, yb_ref, gate_ref, wo_ref, g_ref, b_ref, shift_ref, scale_ref, w_ref,
                       k_ref, v_ref):
    y = _dot(ya_ref[...], wo_ref[0:W, :]) + _dot(yb_ref[...], wo_ref[W:2 * W, :])
    c1 = _residual_ln(x_ref[...], y, gate_ref[...], g_ref[...], b_ref[...])
    h = (c1 * (1.0 + scale_ref[...]) + shift_ref[...]).astype(BF16)
    k_ref[...] = _dot(h, w_ref[:, 0:W]).astype(BF16)
    v_ref[...] = _dot(h, w_ref[:, W:2 * W]).astype(BF16)


def _ctx_out_kv(ctx, ya, yb, gate0, w_out, ln_g, ln_b, shift, scale, w_kv, tm):
    bsz, n, _ = ctx.shape
    tok = lambda width: pl.BlockSpec((None, tm, width), lambda b, i: (b, i, 0))
    return pl.pallas_call(
        _ctx_out_kv_kernel,
        grid=(bsz, n // tm),
        in_specs=[tok(D_MODEL), tok(W), tok(W), _mod_spec(gate0), _bcast_spec(w_out, 2), _bcast_spec(ln_g, 2),
                  _bcast_spec(ln_b, 2), _mod_spec(shift), _mod_spec(scale), _bcast_spec(w_kv, 2)],
        out_specs=[tok(W), tok(W)],
        out_shape=[jax.ShapeDtypeStruct((bsz, n, W), BF16)] * 2,
        compiler_params=_params("parallel", "parallel"),
        name="ctx_out_kv",
    )(ctx, ya, yb, gate0, w_out, ln_g, ln_b, shift, scale, w_kv)


def _rope(t, cos, sin_signed, first_half):
    parts = []
    for j in range(W // LANES):
        tj = t[:, j * LANES:(j + 1) * LANES]
        parts.append(jnp.where(first_half, pltpu.roll(tj, LANES - 16, 1), pltpu.roll(tj, 16, 1)))
    return t * cos + jnp.concatenate(parts, axis=1) * sin_signed


def _ab_out_cd_in_kernel(x_ref, ya_ref, yb_ref, gate_ref, wo_ref, g_ref, b_ref, shift_ref, scale_ref, w_ref,
                         cos_ref, sin_ref, x1_ref, z_ref, cg_ref, q_ref, k_ref, v_ref, ag_ref, *, tm):
    y = _dot(ya_ref[...], wo_ref[0:W, :]) + _dot(yb_ref[...], wo_ref[W:2 * W, :])
    x1 = _residual_ln(x_ref[...], y, gate_ref[...], g_ref[...], b_ref[...])
    x1_ref[...] = x1
    h = (x1 * (1.0 + scale_ref[...]) + shift_ref[...]).astype(BF16)

    def proj(j):
        return _dot(h, w_ref[:, j * W:(j + 1) * W])

    z_ref[...] = proj(2) * proj(0)
    cg_ref[...] = (proj(1) * _silu(proj(3))).astype(BF16)
    cos = cos_ref[...]
    sin = sin_ref[...]
    lane = lax.broadcasted_iota(jnp.int32, (tm, LANES), 1)
    first_half = (lane % 32) < 16
    q_ref[...] = (_rope(proj(4), cos, sin, first_half) * (DIFF_HEAD_DIM ** -0.5 * LOG2_E)).astype(BF16)
    k_ref[...] = _rope(proj(5), cos, sin, first_half).astype(BF16)
    v_ref[...] = proj(6).astype(BF16)
    ag_ref[...] = _silu(proj(7)).astype(BF16)


def _ab_out_cd_in(x, ya, yb, gate0, w_out, ln_g, ln_b, shift, scale, w_in, cos, sin_signed, tm):
    bsz, n, _ = x.shape
    tok = lambda width: pl.BlockSpec((None, tm, width), lambda i, b: (b, i, 0))
    mod = pl.BlockSpec((None, 1, D_MODEL), lambda i, b: (b, 0, 0))
    table = pl.BlockSpec((tm, W), lambda i, b: (i, 0))
    const = lambda a: pl.BlockSpec(a.shape, lambda i, b: (0,) * a.ndim, pipeline_mode=pl.Buffered(1))
    return pl.pallas_call(
        functools.partial(_ab_out_cd_in_kernel, tm=tm),
        grid=(n // tm, bsz),
        in_specs=[tok(D_MODEL), tok(W), tok(W), mod, const(w_out), const(ln_g), const(ln_b), mod, mod, const(w_in),
                  table, table],
        out_specs=[tok(D_MODEL)] + [tok(W)] * 6,
        out_shape=[jax.ShapeDtypeStruct((bsz, n, D_MODEL), F32), jax.ShapeDtypeStruct((bsz, n, W), F32)]
        + [jax.ShapeDtypeStruct((bsz, n, W), BF16)] * 5,
        compiler_params=_params("parallel", "parallel"),
        name="ab_out_cd_in",
    )(x, ya, yb, gate0, w_out, ln_g, ln_b, shift, scale, w_in, cos, sin_signed)


def _attn_out_kernel(q_ref, kc_ref, k_ref, vc_ref, v_ref, ag_ref, x_ref, z_ref, zprev_ref, znext_ref, cg_ref,
                     gate_ref, dl_ref, sg_ref, cw_ref, w_ref, g_ref, b_ref, o_ref,
                     kall_ref, vall_ref, qh_ref, yd_ref, s_ref, p_ref, al_ref, m_ref, acc_ref, *, tq, ck):
    nc = kc_ref.shape[0]
    n = k_ref.shape[0]
    nck = len(ck)
    starts = [sum(ck[:j]) for j in range(nck)]
    i = pl.program_id(1)

    @pl.when(i == 0)
    def _():
        for h in range(DIFF_HEADS):
            cols = slice(h * DIFF_V_DIM, (h + 1) * DIFF_V_DIM)
            kall_ref[h, 0:n, :] = k_ref[:, cols]
            kall_ref[h, n:n + nc, :] = kc_ref[:, cols]
            vall_ref[h, 0:n, 0:DIFF_V_DIM] = v_ref[:, cols]
            vall_ref[h, n:n + nc, 0:DIFF_V_DIM] = vc_ref[:, cols]
            vall_ref[h, :, DIFF_V_DIM:2 * DIFF_V_DIM] = jnp.ones((n + nc, DIFF_V_DIM), BF16)

    dl = dl_ref[...]
    lam = (jnp.exp(jnp.sum(dl[0:1, :] * dl[1:2, :], axis=-1, keepdims=True))
           - jnp.exp(jnp.sum(dl[2:3, :] * dl[3:4, :], axis=-1, keepdims=True)) + LAM_INIT_1)
    lane = lax.broadcasted_iota(jnp.int32, (tq, DIFF_V_DIM), 1)
    for h in range(DIFF_HEADS):
        q = q_ref[:, h * DIFF_V_DIM:(h + 1) * DIFF_V_DIM]
        zero = jnp.zeros_like(q)
        qh_ref[h, 0:tq, :] = jnp.where(lane < DIFF_HEAD_DIM, q, zero)
        qh_ref[h, tq:2 * tq, :] = jnp.where(lane >= DIFF_HEAD_DIM, q, zero)
    nt = (((1,), (1,)), ((), ()))
    slots = s_ref.shape[0]

    def head(h, carry):
        qq = qh_ref[h]

        def scores(j, slot):
            kj = kall_ref[h, starts[j]:starts[j] + ck[j], :]
            s_ref[slot, :, 0:ck[j]] = lax.dot_general(qq, kj, nt, preferred_element_type=F32)

        def softmax(j, slot, first):
            s = s_ref[slot, :, 0:ck[j]]
            sm = s[:, 0:LANES]
            for c in range(1, ck[j] // LANES):
                sm = jnp.maximum(sm, s[:, c * LANES:(c + 1) * LANES])
            m_new = jnp.broadcast_to(jnp.max(sm, axis=-1, keepdims=True), (2 * tq, LANES))
            if not first:
                m_old = m_ref[...]
                m_new = jnp.maximum(m_old, m_new)
                al_ref[slot] = jnp.exp2(m_old - m_new)
            m_ref[...] = m_new
            p_ref[slot, :, 0:ck[j]] = jnp.exp2(s - jnp.concatenate([m_new] * (ck[j] // LANES), axis=1)).astype(BF16)

        def weighted_values(j, slot, first):
            pv = _dot(p_ref[slot, :, 0:ck[j]], vall_ref[h, starts[j]:starts[j] + ck[j], :])
            if first:
                acc_ref[...] = pv
            else:
                alpha = al_ref[slot]
                acc_ref[...] = jnp.concatenate([alpha, alpha], axis=1) * acc_ref[...] + pv

        for t in range(nck + 2):
            if t < nck:
                scores(t, t % slots)
            if 1 <= t <= nck:
                softmax(t - 1, (t - 1) % slots, t == 1)
            if t >= 2:
                weighted_values(t - 2, (t - 2) % slots, t == 2)

        acc = acc_ref[:, 0:DIFF_V_DIM]
        w = 1.0 / acc_ref[:, DIFF_V_DIM:2 * DIFF_V_DIM]
        o = acc[0:tq, :] * w[0:tq, :] - acc[tq:2 * tq, :] * (lam * w[tq:2 * tq, :])
        yd_ref[h] = (o * lax.rsqrt(jnp.mean(o * o, axis=-1, keepdims=True) + LN_EPS)) * sg_ref[...] * (1.0 - LAM_INIT_1)
        return carry

    lax.fori_loop(0, DIFF_HEADS, head, 0)

    yd = jnp.concatenate([yd_ref[h] for h in range(DIFF_HEADS)], axis=1) * ag_ref[...].astype(F32)
    z = z_ref[...]
    row = lax.broadcasted_iota(jnp.int32, (tq, W), 0)
    prev_row = jnp.where(i > 0, zprev_ref[SUBLANES - 1:SUBLANES, :], 0.0)
    next_row = jnp.where(i < pl.num_programs(1) - 1, znext_ref[0:1, :], 0.0)
    z_m1 = jnp.where(row == 0, prev_row, pltpu.roll(z, 1, 0))
    z_p1 = jnp.where(row == tq - 1, next_row, pltpu.roll(z, tq - 1, 0))
    cw = cw_ref[...]
    conv = cw[0:1, :] * z_m1 + cw[1:2, :] * z + cw[2:3, :] * z_p1
    yc = (cg_ref[...].astype(F32) * conv).astype(BF16)
    y = _dot(yc, w_ref[0:W, :]) + _dot(yd.astype(BF16), w_ref[W:2 * W, :])
    o_ref[...] = _residual_ln(x_ref[...], y, gate_ref[...], g_ref[...], b_ref[...])


def _attention_out(q, kc, k, vc, v, ag, x, z, cg, gate, diff_lambda, subln_g, sconv_w, w_out, ln_g, ln_b, tq, ck):
    bsz, n, _ = q.shape
    nc = kc.shape[1]
    assert sum(ck) == n + nc and all(c % (2 * LANES) == 0 for c in ck)
    tok = lambda width: pl.BlockSpec((None, tq, width), lambda b, i: (b, i, 0))
    keys = lambda length: pl.BlockSpec((None, length, W), lambda b, i: (b, 0, 0))
    per = tq // SUBLANES
    last = n // SUBLANES - 1
    zprev = pl.BlockSpec((None, SUBLANES, W), lambda b, i: (b, jnp.maximum(i * per - 1, 0), 0))
    znext = pl.BlockSpec((None, SUBLANES, W), lambda b, i: (b, jnp.minimum((i + 1) * per, last), 0))
    const = lambda a: pl.BlockSpec(a.shape, lambda b, i: (0,) * a.ndim, pipeline_mode=pl.Buffered(1))
    consts = [diff_lambda, subln_g, sconv_w, w_out, ln_g, ln_b]
    slots = 2
    return pl.pallas_call(
        functools.partial(_attn_out_kernel, tq=tq, ck=ck),
        grid=(bsz, n // tq),
        in_specs=[tok(W), keys(nc), keys(n), keys(nc), keys(n), tok(W), tok(D_MODEL), tok(W), zprev, znext, tok(W),
                  _mod_spec(gate)] + [const(a) for a in consts],
        out_specs=tok(D_MODEL),
        out_shape=jax.ShapeDtypeStruct((bsz, n, D_MODEL), F32),
        scratch_shapes=[pltpu.VMEM((DIFF_HEADS, n + nc, DIFF_V_DIM), BF16),
                        pltpu.VMEM((DIFF_HEADS, n + nc, 2 * DIFF_V_DIM), BF16),
                        pltpu.VMEM((DIFF_HEADS, 2 * tq, DIFF_V_DIM), BF16),
                        pltpu.VMEM((DIFF_HEADS, tq, DIFF_V_DIM), F32),
                        pltpu.VMEM((slots, 2 * tq, max(ck)), F32), pltpu.VMEM((slots, 2 * tq, max(ck)), BF16),
                        pltpu.VMEM((slots, 2 * tq, LANES), F32), pltpu.VMEM((2 * tq, LANES), F32),
                        pltpu.VMEM((2 * tq, 2 * DIFF_V_DIM), F32)],
        compiler_params=_params("parallel", "arbitrary"),
        name="attention_out",
    )(q, kc, k, vc, v, ag, x, z, z, z, cg, gate, *consts)


def _pair_gate_weights(w_a, w_x):
    def pair_blockdiag(w):
        wp = w.reshape(N_PAIR, 2, LRU_HEAD_DIM, LRU_HEAD_DIM)
        zero = jnp.zeros_like(wp[:, 0])
        top = jnp.concatenate([wp[:, 0], zero], axis=-1)
        bot = jnp.concatenate([zero, wp[:, 1]], axis=-1)
        return jnp.concatenate([top, bot], axis=-2)
    return jnp.concatenate([pair_blockdiag(w_a[0]), pair_blockdiag(w_x[0]),
                            pair_blockdiag(w_a[1]), pair_blockdiag(w_x[1])], axis=-1).astype(BF16)


def _rope_tables(n):
    rows = n // GRID_W
    row = jnp.repeat(jnp.arange(rows, dtype=F32), GRID_W)
    col = jnp.tile(jnp.arange(GRID_W, dtype=F32), rows)
    n_freq = DIFF_HEAD_DIM // 4
    inv = ROPE_BASE ** (-jnp.arange(n_freq, dtype=F32) / n_freq)
    ang_r = row[:, None] * inv
    ang_c = col[:, None] * inv
    ang = jnp.concatenate([ang_r, ang_r, ang_c, ang_c], axis=-1)
    sign = jnp.tile(jnp.concatenate([-jnp.ones((n_freq,), F32), jnp.ones((n_freq,), F32)]), 2)
    reps = W // DIFF_HEAD_DIM
    return jnp.tile(jnp.cos(ang), (1, reps)), jnp.tile(jnp.sin(ang) * sign, (1, reps))


def kernel(x, c, ctx, c_ctx, w_mod, b_mod, ln_g, ln_b, ab_w_in, ab_w_out, lru_conv_w, lru_conv_b, lru_w_a, lru_b_a,
           lru_w_x, lru_b_x, lru_lambda, sgu_ln_g, sgu_ln_b, sgu_w, sgu_b, cd_w_in, cd_w_out, sconv_w, diff_lambda,
           diff_subln_g):
    bsz, n, _ = x.shape
    nc = ctx.shape[1]
    row = lambda a: a.reshape(1, -1)

    n_cond = -(-(bsz + 1) // SUBLANES) * SUBLANES
    cond = jnp.zeros((n_cond, D_MODEL), F32).at[:bsz].set(c).at[bsz].set(c_ctx)
    mod = _modulation(cond, w_mod, b_mod)

    def mods(l):
        lat = [mod[l, :bsz, j * D_MODEL:(j + 1) * D_MODEL][:, None, :] for j in range(3)]
        con = [mod[l, bsz:bsz + 1, j * D_MODEL:(j + 1) * D_MODEL][:, None, :] for j in range(3)]
        return lat, con

    (shift, scale, gate), (shift_c, scale_c, gate_c) = mods(0)
    w_in = ab_w_in[0].astype(BF16)
    w_out = ab_w_out[0].astype(BF16)
    sgu_args = (row(sgu_ln_g[0]), row(sgu_ln_b[0]), sgu_w[0].astype(BF16),
                jnp.broadcast_to(sgu_b[0][:, :, None], (SGU_GROUPS, SGU_CHUNK, LANES)))
    lx, ga, yb = _ab_in(x, shift, scale, w_in, *sgu_args, tm=512)
    lxc, gac, ybc = _ab_in(ctx, shift_c, scale_c, w_in, *sgu_args, tm=nc)
    wpair = _pair_gate_weights(lru_w_a[0], lru_w_x[0])
    ya, yac = _lru(lx, lxc, ga, gac, lru_conv_w[0], row(lru_conv_b[0]), wpair, lru_b_a[0], lru_b_x[0], lru_lambda[0])

    (shift, scale, gate1), (shift_c, scale_c, _) = mods(1)
    w_in = cd_w_in[0].astype(BF16)
    cos, sin_signed = _rope_tables(n)
    x1, z, cg, q, k, v, ag = _ab_out_cd_in(x, ya, yb, gate, w_out, row(ln_g[0]), row(ln_b[0]), shift, scale, w_in,
                                           cos, sin_signed, tm=512)
    kc, vc = _ctx_out_kv(ctx, yac, ybc, gate_c, w_out, row(ln_g[0]), row(ln_b[0]), shift_c, scale_c,
                         w_in[:, 5 * W:7 * W], tm=nc)
    gate = gate1
    w_out = cd_w_out[0].astype(BF16)
    return _attention_out(q, kc, k, vc, v, ag, x1, z, cg, gate, diff_lambda[0], row(diff_subln_g[0]), sconv_w[0], w_out,
                          row(ln_g[1]), row(ln_b[1]), tq=512, ck=(1024, 1024, 256))
```

```python
import functools
import math

import jax
import jax.numpy as jnp
from jax import lax
from jax.experimental import pallas as pl
from jax.experimental.pallas import tpu as pltpu

D_MODEL = 1024
DEPTH = 2
GRID_W = 64
W = D_MODEL // 2
LRU_HEAD_DIM = 64
LRU_C = 8.0
LRU_PAIR = 2 * LRU_HEAD_DIM
N_PAIR = W // LRU_PAIR
SGU_CHUNK = 128
SGU_GROUPS = 4
DIFF_V_DIM = 128
DIFF_HEADS = W // DIFF_V_DIM
DIFF_HEAD_DIM = DIFF_V_DIM // 2
ROPE_BASE = 10000.0
DEEPNORM_ALPHA = (2 * DEPTH) ** 0.25
LN_EPS = 1e-5
LAM_INIT_1 = 0.8 - 0.6 * math.exp(-0.3 * 1)
LOG2_E = math.log2(math.e)

SUBLANES = 8
LANES = 128
VMEM_LIMIT = 56 * 1024 * 1024

F32 = jnp.float32
BF16 = jnp.bfloat16


def _silu(x):
    return x * jax.nn.sigmoid(x)


def _dot(a, b):
    return jnp.dot(a, b, preferred_element_type=F32)


def _params(*sem):
    return pltpu.CompilerParams(dimension_semantics=sem, vmem_limit_bytes=VMEM_LIMIT)


def _bcast_spec(arr, nd_grid):
    zeros = (0,) * arr.ndim
    return pl.BlockSpec(arr.shape, lambda *_: zeros)


def _mod_spec(arr):
    if arr.shape[0] == 1:
        return pl.BlockSpec((None, 1, D_MODEL), lambda b, i: (0, 0, 0))
    return pl.BlockSpec((None, 1, D_MODEL), lambda b, i: (b, 0, 0))


def _mod_kernel(cond_ref, w_ref, b_ref, o_ref):
    s = _silu(cond_ref[...]).astype(BF16)
    o_ref[...] = _dot(s, w_ref[...].astype(BF16)) + b_ref[...]


def _modulation(cond, w_mod, b_mod):
    n = cond.shape[0]
    return pl.pallas_call(
        _mod_kernel,
        grid=(DEPTH, 3),
        in_specs=[
            pl.BlockSpec((n, D_MODEL), lambda l, j: (0, 0)),
            pl.BlockSpec((None, D_MODEL, D_MODEL), lambda l, j: (l, 0, j)),
            pl.BlockSpec((None, 1, D_MODEL), lambda l, j: (l, 0, j)),
        ],
        out_specs=pl.BlockSpec((None, n, D_MODEL), lambda l, j: (l, 0, j)),
        out_shape=jax.ShapeDtypeStruct((DEPTH, n, 3 * D_MODEL), F32),
        compiler_params=_params("arbitrary", "arbitrary"),
        name="modulation",
    )(cond, w_mod, b_mod.reshape(DEPTH, 1, 3 * D_MODEL))


def _ab_in_kernel(x_ref, shift_ref, scale_ref, w_ref, lng_ref, lnb_ref, ws_ref, bs_ref,
                  lx_ref, ga_ref, yb_ref, *, tm):
    h = (x_ref[...] * (1.0 + scale_ref[...]) + shift_ref[...]).astype(BF16)

    def proj(j):
        return _dot(h, w_ref[:, j * W:(j + 1) * W])

    lx_ref[...] = proj(0)
    ga_ref[...] = _silu(proj(1)).astype(BF16)
    u = jax.nn.gelu(proj(2))
    v = jax.nn.gelu(proj(3))
    mu = jnp.mean(v, axis=-1, keepdims=True)
    vc = v - mu
    var = jnp.mean(vc * vc, axis=-1, keepdims=True)
    vn = ((vc * lax.rsqrt(var + LN_EPS)) * lng_ref[...] + lnb_ref[...]).astype(BF16)
    ug = u * _silu(proj(4))
    for c in range(tm // SGU_CHUNK):
        rows = slice(c * SGU_CHUNK, (c + 1) * SGU_CHUNK)
        for g in range(SGU_GROUPS):
            cols = slice(g * LANES, (g + 1) * LANES)
            s = _dot(ws_ref[g], vn[rows, cols]) + bs_ref[g]
            yb_ref[rows, cols] = (ug[rows, cols] * s).astype(BF16)


def _ab_in(x, shift, scale, w_in, sgu_ln_g, sgu_ln_b, sgu_w, sgu_b, tm):
    bsz, n, _ = x.shape
    tok = lambda width: pl.BlockSpec((None, tm, width), lambda b, i: (b, i, 0))
    consts = [w_in, sgu_ln_g, sgu_ln_b, sgu_w, sgu_b]
    return pl.pallas_call(
        functools.partial(_ab_in_kernel, tm=tm),
        grid=(bsz, n // tm),
        in_specs=[tok(D_MODEL), _mod_spec(shift), _mod_spec(scale)] + [_bcast_spec(a, 2) for a in consts],
        out_specs=[tok(W), tok(W), tok(W)],
        out_shape=[jax.ShapeDtypeStruct((bsz, n, W), F32),
                   jax.ShapeDtypeStruct((bsz, n, W), BF16),
                   jax.ShapeDtypeStruct((bsz, n, W), BF16)],
        compiler_params=_params("parallel", "parallel"),
        name="ab_in_sgu",
    )(x, shift, scale, *consts)


SCAN_SEGMENTS = SUBLANES
SCAN_BLOCK = 32


def _lru_kernel(lx_ref, lxc_ref, ga_ref, gac_ref, cw_ref, cb_ref, wp_ref, ba_ref, bx_ref, lam_ref,
                ya_ref, yac_ref, xp_ref, xpc_ref, af_ref, bf_ref, ar_ref, br_ref, hf_ref, hr_ref, *, n, nc, tr):
    nt = n + nc
    lseg = nt // SCAN_SEGMENTS
    pad = SUBLANES
    wl = lx_ref.shape[-1]
    npair = wl // LANES
    zeros = jnp.zeros((pad, wl), F32)
    for ref, src, length in ((xp_ref, lx_ref, n), (xpc_ref, lxc_ref, nc)):
        ref[0:pad, :] = zeros
        ref[pad + length:2 * pad + length, :] = zeros
        ref[pad:pad + length, :] = src[...]

    lam = lam_ref[...]
    neg = -lam
    softplus = jnp.maximum(neg, 0.0) + jnp.log1p(jnp.exp(-jnp.abs(neg)))
    rate = LRU_C * softplus
    rate_log2 = -LOG2_E * rate
    cw = cw_ref[...]
    cb = cb_ref[...]

    def block_rows(pos):
        if isinstance(pos, int):
            seg, step = divmod(pos, lseg)
        else:
            seg = lax.div(pos, lseg)
            step = pos - seg * lseg
        return pl.ds(step * SCAN_SEGMENTS + seg, SCAN_BLOCK, stride=SCAN_SEGMENTS)

    def coeffs(src_ref, r0, f_off, r_off):
        xw = src_ref[pl.ds(r0, tr + 2 * pad), :]
        xc = cb + sum(cw[k:k + 1, :] * xw[pad - 2 + k:pad - 2 + k + tr, :] for k in range(4))
        xcb = xc.astype(BF16)
        dst = [[block_rows(off + r0 + blk) for blk in range(0, tr, SCAN_BLOCK)] for off in (f_off, r_off)]
        for j in range(npair):
            cols = slice(j * LANES, (j + 1) * LANES)
            gts = _dot(xcb[:, cols], wp_ref[j])
            xj = xc[:, cols]
            for d, (a_ref, b_ref) in enumerate(((af_ref, bf_ref), (ar_ref, br_ref))):
                gate_r = jax.nn.sigmoid(gts[:, (2 * d) * LANES:(2 * d + 1) * LANES] + ba_ref[d:d + 1, cols])
                gate_i = jax.nn.sigmoid(gts[:, (2 * d + 1) * LANES:(2 * d + 2) * LANES] + bx_ref[d:d + 1, cols])
                a = jnp.exp2(gate_r * rate_log2[d:d + 1, cols])
                one_minus_a2 = jnp.tanh(gate_r * rate[d:d + 1, cols]) * (1.0 + a * a)
                root = jnp.where(one_minus_a2 > 0.0, one_minus_a2 * lax.rsqrt(one_minus_a2), 0.0)
                b = root * (gate_i * xj)
                for rows, blk in zip(dst[d], range(0, tr, SCAN_BLOCK)):
                    a_ref[j, rows, :] = a[blk:blk + SCAN_BLOCK, :]
                    b_ref[j, rows, :] = b[blk:blk + SCAN_BLOCK, :]

    def lat_tile(i, carry):
        coeffs(xp_ref, pl.multiple_of(i * tr, tr), nc, 0)
        return carry

    lax.fori_loop(0, n // tr, lat_tile, 0)
    for i in range(nc // tr):
        coeffs(xpc_ref, i * tr, 0, n)

    chains = [(a_ref, b_ref, h_ref, j, rev)
              for a_ref, b_ref, h_ref, rev in ((af_ref, bf_ref, hf_ref, False), (ar_ref, br_ref, hr_ref, True))
              for j in range(npair)]

    def step_rows(t, rev):
        return pl.ds(pl.multiple_of((lseg - 1 - t if rev else t) * SCAN_SEGMENTS, SCAN_SEGMENTS), SCAN_SEGMENTS)

    def totals_step(t, carry):
        out = []
        for (a_ref, b_ref, _, j, rev), (decay, h) in zip(chains, carry):
            rows = step_rows(t, rev)
            a = a_ref[j, rows, :]
            out.append((a * decay, a * h + b_ref[j, rows, :]))
        return tuple(out)

    one = jnp.ones((SCAN_SEGMENTS, LANES), F32)
    zero = jnp.zeros((SCAN_SEGMENTS, LANES), F32)
    totals = lax.fori_loop(0, lseg, totals_step, tuple((one, zero) for _ in chains), unroll=8)

    def entering_state(decay, h_end, rev):
        order = range(SCAN_SEGMENTS - 1, -1, -1) if rev else range(SCAN_SEGMENTS)
        rows = [None] * SCAN_SEGMENTS
        state = jnp.zeros((1, LANES), F32)
        for s in order:
            rows[s] = state
            state = decay[s:s + 1, :] * state + h_end[s:s + 1, :]
        return jnp.concatenate(rows, axis=0)

    starts = tuple(entering_state(decay, h_end, chain[-1]) for chain, (decay, h_end) in zip(chains, totals))

    def states_step(t, carry):
        out = []
        for (a_ref, b_ref, h_ref, j, rev), h in zip(chains, carry):
            rows = step_rows(t, rev)
            h = a_ref[j, rows, :] * h + b_ref[j, rows, :]
            h_ref[j, rows, :] = h
            out.append(h)
        return tuple(out)

    lax.fori_loop(0, lseg, states_step, starts, unroll=8)

    def write_out(y_ref, g_ref, r0, f_pos, r_pos, rows):
        for blk in range(0, rows, SCAN_BLOCK):
            dst = pl.ds(r0 + blk, SCAN_BLOCK)
            rows_f = block_rows(f_pos + blk)
            rows_r = block_rows(r_pos + blk)
            for j in range(npair):
                cols = slice(j * LANES, (j + 1) * LANES)
                h = hf_ref[j, rows_f, :] + hr_ref[j, rows_r, :]
                y_ref[dst, cols] = (h * g_ref[dst, cols].astype(F32)).astype(BF16)

    def out_tile(i, carry):
        r0 = pl.multiple_of(i * tr, tr)
        write_out(ya_ref, ga_ref, r0, nc + r0, r0, tr)
        return carry

    lax.fori_loop(0, n // tr, out_tile, 0)
    write_out(yac_ref, gac_ref, 0, 0, n, nc)


def _lru(lx, lxc, ga, gac, conv_w, conv_b, wpair, b_a, b_x, lam, tr=256):
    bsz, n, _ = lx.shape
    nc = lxc.shape[1]
    lseg, rem = divmod(n + nc, SCAN_SEGMENTS)
    assert rem == 0 and lseg % SCAN_BLOCK == 0 and n % tr == 0 and nc % tr == 0 and tr % SCAN_BLOCK == 0
    npair = 2
    wl = npair * LANES
    seq = lambda length: pl.BlockSpec((None, length, wl), lambda b, g: (b, 0, g))
    vec = lambda rows: pl.BlockSpec((rows, wl), lambda b, g: (0, g))
    pad = 2 * SUBLANES
    coeff = pltpu.VMEM((npair, SCAN_SEGMENTS * lseg, LANES), F32)
    return pl.pallas_call(
        functools.partial(_lru_kernel, n=n, nc=nc, tr=tr),
        grid=(bsz, N_PAIR // npair),
        in_specs=[seq(n), seq(nc), seq(n), seq(nc), vec(conv_w.shape[0]), vec(1),
                  pl.BlockSpec((npair,) + wpair.shape[1:], lambda b, g: (g, 0, 0)), vec(2), vec(2), vec(2)],
        out_specs=[seq(n), seq(nc)],
        out_shape=[jax.ShapeDtypeStruct((bsz, n, W), BF16), jax.ShapeDtypeStruct((bsz, nc, W), BF16)],
        scratch_shapes=[pltpu.VMEM((n + pad, wl), F32), pltpu.VMEM((nc + pad, wl), F32)] + [coeff] * 6,
        compiler_params=_params("parallel", "parallel"),
        name="rglru",
    )(lx, lxc, ga, gac, conv_w, conv_b, wpair, b_a, b_x, lam)


def _residual_ln(x, y, gate, g, b):
    z = DEEPNORM_ALPHA * x + gate * y
    mu = jnp.mean(z, axis=-1, keepdims=True)
    zc = z - mu
    var = jnp.mean(zc * zc, axis=-1, keepdims=True)
    return (zc * lax.rsqrt(var + LN_EPS)) * g + b


def _ctx_out_kv_kernel(x_ref, ya_ref, yb_ref, gate_ref, wo_ref, g_ref, b_ref, shift_ref, scale_ref, w_ref,
                       k_ref, v_ref):
    y = _dot(ya_ref[...], wo_ref[0:W, :]) + _dot(yb_ref[...], wo_ref[W:2 * W, :])
    c1 = _residual_ln(x_ref[...], y, gate_ref[...], g_ref[...], b_ref[...])
    h = (c1 * (1.0 + scale_ref[...]) + shift_ref[...]).astype(BF16)
    k_ref[...] = _dot(h, w_ref[:, 0:W]).astype(BF16)
    v_ref[...] = _dot(h, w_ref[:, W:2 * W]).astype(BF16)


def _ctx_out_kv(ctx, ya, yb, gate0, w_out, ln_g, ln_b, shift, scale, w_kv, tm):
    bsz, n, _ = ctx.shape
    tok = lambda width: pl.BlockSpec((None, tm, width), lambda b, i: (b, i, 0))
    return pl.pallas_call(
        _ctx_out_kv_kernel,
        grid=(bsz, n // tm),
        in_specs=[tok(D_MODEL), tok(W), tok(W), _mod_spec(gate0), _bcast_spec(w_out, 2), _bcast_spec(ln_g, 2),
                  _bcast_spec(ln_b, 2), _mod_spec(shift), _mod_spec(scale), _bcast_spec(w_kv, 2)],
        out_specs=[tok(W), tok(W)],
        out_shape=[jax.ShapeDtypeStruct((bsz, n, W), BF16)] * 2,
        compiler_params=_params("parallel", "parallel"),
        name="ctx_out_kv",
    )(ctx, ya, yb, gate0, w_out, ln_g, ln_b, shift, scale, w_kv)


def _rope(t, cos, sin_signed, first_half):
    parts = []
    for j in range(W // LANES):
        tj = t[:, j * LANES:(j + 1) * LANES]
        parts.append(jnp.where(first_half, pltpu.roll(tj, LANES - 16, 1), pltpu.roll(tj, 16, 1)))
    return t * cos + jnp.concatenate(parts, axis=1) * sin_signed


def _ab_out_cd_in_kernel(x_ref, ya_ref, yb_ref, gate_ref, wo_ref, g_ref, b_ref, shift_ref, scale_ref, w_ref,
                         cos_ref, sin_ref, x1_ref, z_ref, cg_ref, q_ref, k_ref, v_ref, ag_ref, *, tm):
    y = _dot(ya_ref[...], wo_ref[0:W, :]) + _dot(yb_ref[...], wo_ref[W:2 * W, :])
    x1 = _residual_ln(x_ref[...], y, gate_ref[...], g_ref[...], b_ref[...])
    x1_ref[...] = x1
    h = (x1 * (1.0 + scale_ref[...]) + shift_ref[...]).astype(BF16)

    def proj(j):
        return _dot(h, w_ref[:, j * W:(j + 1) * W])

    z_ref[...] = proj(2) * proj(0)
    cg_ref[...] = (proj(1) * _silu(proj(3))).astype(BF16)
    cos = cos_ref[...]
    sin = sin_ref[...]
    lane = lax.broadcasted_iota(jnp.int32, (tm, LANES), 1)
    first_half = (lane % 32) < 16
    q_ref[...] = (_rope(proj(4), cos, sin, first_half) * (DIFF_HEAD_DIM ** -0.5 * LOG2_E)).astype(BF16)
    k_ref[...] = _rope(proj(5), cos, sin, first_half).astype(BF16)
    v_ref[...] = proj(6).astype(BF16)
    ag_ref[...] = _silu(proj(7)).astype(BF16)


def _ab_out_cd_in(x, ya, yb, gate0, w_out, ln_g, ln_b, shift, scale, w_in, cos, sin_signed, tm):
    bsz, n, _ = x.shape
    tok = lambda width: pl.BlockSpec((None, tm, width), lambda i, b: (b, i, 0))
    mod = pl.BlockSpec((None, 1, D_MODEL), lambda i, b: (b, 0, 0))
    table = pl.BlockSpec((tm, W), lambda i, b: (i, 0))
    const = lambda a: pl.BlockSpec(a.shape, lambda i, b: (0,) * a.ndim, pipeline_mode=pl.Buffered(1))
    return pl.pallas_call(
        functools.partial(_ab_out_cd_in_kernel, tm=tm),
        grid=(n // tm, bsz),
        in_specs=[tok(D_MODEL), tok(W), tok(W), mod, const(w_out), const(ln_g), const(ln_b), mod, mod, const(w_in),
                  table, table],
        out_specs=[tok(D_MODEL)] + [tok(W)] * 6,
        out_shape=[jax.ShapeDtypeStruct((bsz, n, D_MODEL), F32), jax.ShapeDtypeStruct((bsz, n, W), F32)]
        + [jax.ShapeDtypeStruct((bsz, n, W), BF16)] * 5,
        compiler_params=_params("parallel", "parallel"),
        name="ab_out_cd_in",
    )(x, ya, yb, gate0, w_out, ln_g, ln_b, shift, scale, w_in, cos, sin_signed)


def _attn_out_kernel(q_ref, kc_ref, k_ref, vc_ref, v_ref, ag_ref, x_ref, z_ref, zprev_ref, znext_ref, cg_ref,
                     gate_ref, dl_ref, sg_ref, cw_ref, w_ref, g_ref, b_ref, o_ref,
                     kall_ref, vall_ref, qh_ref, yd_ref, s_ref, p_ref, al_ref, m_ref, acc_ref, *, tq, ck):
    nc = kc_ref.shape[0]
    n = k_ref.shape[0]
    nck = len(ck)
    starts = [sum(ck[:j]) for j in range(nck)]
    i = pl.program_id(1)

    @pl.when(i == 0)
    def _():
        for h in range(DIFF_HEADS):
            cols = slice(h * DIFF_V_DIM, (h + 1) * DIFF_V_DIM)
            kall_ref[h, 0:n, :] = k_ref[:, cols]
            kall_ref[h, n:n + nc, :] = kc_ref[:, cols]
            vall_ref[h, 0:n, 0:DIFF_V_DIM] = v_ref[:, cols]
            vall_ref[h, n:n + nc, 0:DIFF_V_DIM] = vc_ref[:, cols]
            vall_ref[h, :, DIFF_V_DIM:2 * DIFF_V_DIM] = jnp.ones((n + nc, DIFF_V_DIM), BF16)

    dl = dl_ref[...]
    lam = (jnp.exp(jnp.sum(dl[0:1, :] * dl[1:2, :], axis=-1, keepdims=True))
           - jnp.exp(jnp.sum(dl[2:3, :] * dl[3:4, :], axis=-1, keepdims=True)) + LAM_INIT_1)
    lane = lax.broadcasted_iota(jnp.int32, (tq, DIFF_V_DIM), 1)
    for h in range(DIFF_HEADS):
        q = q_ref[:, h * DIFF_V_DIM:(h + 1) * DIFF_V_DIM]
        zero = jnp.zeros_like(q)
        qh_ref[h, 0:tq, :] = jnp.where(lane < DIFF_HEAD_DIM, q, zero)
        qh_ref[h, tq:2 * tq, :] = jnp.where(lane >= DIFF_HEAD_DIM, q, zero)
    nt = (((1,), (1,)), ((), ()))
    slots = s_ref.shape[0]

    def head(h, carry):
        qq = qh_ref[h]

        def scores(j, slot):
            kj = kall_ref[h, starts[j]:starts[j] + ck[j], :]
            s_ref[slot, :, 0:ck[j]] = lax.dot_general(qq, kj, nt, preferred_element_type=F32)

        def softmax(j, slot, first):
            s = s_ref[slot, :, 0:ck[j]]
            sm = s[:, 0:LANES]
            for c in range(1, ck[j] // LANES):
                sm = jnp.maximum(sm, s[:, c * LANES:(c + 1) * LANES])
            m_new = jnp.broadcast_to(jnp.max(sm, axis=-1, keepdims=True), (2 * tq, LANES))
            if not first:
                m_old = m_ref[...]
                m_new = jnp.maximum(m_old, m_new)
                al_ref[slot] = jnp.exp2(m_old - m_new)
            m_ref[...] = m_new
            p_ref[slot, :, 0:ck[j]] = jnp.exp2(s - jnp.concatenate([m_new] * (ck[j] // LANES), axis=1)).astype(BF16)

        def weighted_values(j, slot, first):
            pv = _dot(p_ref[slot, :, 0:ck[j]], vall_ref[h, starts[j]:starts[j] + ck[j], :])
            if first:
                acc_ref[...] = pv
            else:
                alpha = al_ref[slot]
                acc_ref[...] = jnp.concatenate([alpha, alpha], axis=1) * acc_ref[...] + pv

        for t in range(nck + 2):
            if t < nck:
                scores(t, t % slots)
            if 1 <= t <= nck:
                softmax(t - 1, (t - 1) % slots, t == 1)
            if t >= 2:
                weighted_values(t - 2, (t - 2) % slots, t == 2)

        acc = acc_ref[:, 0:DIFF_V_DIM]
        w = 1.0 / acc_ref[:, DIFF_V_DIM:2 * DIFF_V_DIM]
        o = acc[0:tq, :] * w[0:tq, :] - acc[tq:2 * tq, :] * (lam * w[tq:2 * tq, :])
        yd_ref[h] = (o * lax.rsqrt(jnp.mean(o * o, axis=-1, keepdims=True) + LN_EPS)) * sg_ref[...] * (1.0 - LAM_INIT_1)
        return carry

    lax.fori_loop(0, DIFF_HEADS, head, 0, unroll=True)

    yd = jnp.concatenate([yd_ref[h] for h in range(DIFF_HEADS)], axis=1) * ag_ref[...].astype(F32)
    z = z_ref[...]
    row = lax.broadcasted_iota(jnp.int32, (tq, W), 0)
    prev_row = jnp.where(i > 0, zprev_ref[SUBLANES - 1:SUBLANES, :], 0.0)
    next_row = jnp.where(i < pl.num_programs(1) - 1, znext_ref[0:1, :], 0.0)
    z_m1 = jnp.where(row == 0, prev_row, pltpu.roll(z, 1, 0))
    z_p1 = jnp.where(row == tq - 1, next_row, pltpu.roll(z, tq - 1, 0))
    cw = cw_ref[...]
    conv = cw[0:1, :] * z_m1 + cw[1:2, :] * z + cw[2:3, :] * z_p1
    yc = (cg_ref[...].astype(F32) * conv).astype(BF16)
    y = _dot(yc, w_ref[0:W, :]) + _dot(yd.astype(BF16), w_ref[W:2 * W, :])
    o_ref[...] = _residual_ln(x_ref[...], y, gate_ref[...], g_ref[...], b_ref[...])


def _attention_out(q, kc, k, vc, v, ag, x, z, cg, gate, diff_lambda, subln_g, sconv_w, w_out, ln_g, ln_b, tq, ck):
    bsz, n, _ = q.shape
    nc = kc.shape[1]
    assert sum(ck) == n + nc and all(c % (2 * LANES) == 0 for c in ck)
    tok = lambda width: pl.BlockSpec((None, tq, width), lambda b, i: (b, i, 0))
    keys = lambda length: pl.BlockSpec((None, length, W), lambda b, i: (b, 0, 0))
    per = tq // SUBLANES
    last = n // SUBLANES - 1
    zprev = pl.BlockSpec((None, SUBLANES, W), lambda b, i: (b, jnp.maximum(i * per - 1, 0), 0))
    znext = pl.BlockSpec((None, SUBLANES, W), lambda b, i: (b, jnp.minimum((i + 1) * per, last), 0))
    const = lambda a: pl.BlockSpec(a.shape, lambda b, i: (0,) * a.ndim, pipeline_mode=pl.Buffered(1))
    consts = [diff_lambda, subln_g, sconv_w, w_out, ln_g, ln_b]
    slots = 2
    return pl.pallas_call(
        functools.partial(_attn_out_kernel, tq=tq, ck=ck),
        grid=(bsz, n // tq),
        in_specs=[tok(W), keys(nc), keys(n), keys(nc), keys(n), tok(W), tok(D_MODEL), tok(W), zprev, znext, tok(W),
                  _mod_spec(gate)] + [const(a) for a in consts],
        out_specs=tok(D_MODEL),
        out_shape=jax.ShapeDtypeStruct((bsz, n, D_MODEL), F32),
        scratch_shapes=[pltpu.VMEM((DIFF_HEADS, n + nc, DIFF_V_DIM), BF16),
                        pltpu.VMEM((DIFF_HEADS, n + nc, 2 * DIFF_V_DIM), BF16),
                        pltpu.VMEM((DIFF_HEADS, 2 * tq, DIFF_V_DIM), BF16),
                        pltpu.VMEM((DIFF_HEADS, tq, DIFF_V_DIM), F32),
                        pltpu.VMEM((slots, 2 * tq, max(ck)), F32), pltpu.VMEM((slots, 2 * tq, max(ck)), BF16),
                        pltpu.VMEM((slots, 2 * tq, LANES), F32), pltpu.VMEM((2 * tq, LANES), F32),
                        pltpu.VMEM((2 * tq, 2 * DIFF_V_DIM), F32)],
        compiler_params=_params("parallel", "arbitrary"),
        name="attention_out",
    )(q, kc, k, vc, v, ag, x, z, z, z, cg, gate, *consts)


def _pair_gate_weights(w_a, w_x):
    def pair_blockdiag(w):
        wp = w.reshape(N_PAIR, 2, LRU_HEAD_DIM, LRU_HEAD_DIM)
        zero = jnp.zeros_like(wp[:, 0])
        top = jnp.concatenate([wp[:, 0], zero], axis=-1)
        bot = jnp.concatenate([zero, wp[:, 1]], axis=-1)
        return jnp.concatenate([top, bot], axis=-2)
    return jnp.concatenate([pair_blockdiag(w_a[0]), pair_blockdiag(w_x[0]),
                            pair_blockdiag(w_a[1]), pair_blockdiag(w_x[1])], axis=-1).astype(BF16)


def _rope_tables(n):
    rows = n // GRID_W
    row = jnp.repeat(jnp.arange(rows, dtype=F32), GRID_W)
    col = jnp.tile(jnp.arange(GRID_W, dtype=F32), rows)
    n_freq = DIFF_HEAD_DIM // 4
    inv = ROPE_BASE ** (-jnp.arange(n_freq, dtype=F32) / n_freq)
    ang_r = row[:, None] * inv
    ang_c = col[:, None] * inv
    ang = jnp.concatenate([ang_r, ang_r, ang_c, ang_c], axis=-1)
    sign = jnp.tile(jnp.concatenate([-jnp.ones((n_freq,), F32), jnp.ones((n_freq,), F32)]), 2)
    reps = W // DIFF_HEAD_DIM
    return jnp.tile(jnp.cos(ang), (1, reps)), jnp.tile(jnp.sin(ang) * sign, (1, reps))


def kernel(x, c, ctx, c_ctx, w_mod, b_mod, ln_g, ln_b, ab_w_in, ab_w_out, lru_conv_w, lru_conv_b, lru_w_a, lru_b_a,
           lru_w_x, lru_b_x, lru_lambda, sgu_ln_g, sgu_ln_b, sgu_w, sgu_b, cd_w_in, cd_w_out, sconv_w, diff_lambda,
           diff_subln_g):
    bsz, n, _ = x.shape
    nc = ctx.shape[1]
    row = lambda a: a.reshape(1, -1)

    n_cond = -(-(bsz + 1) // SUBLANES) * SUBLANES
    cond = jnp.zeros((n_cond, D_MODEL), F32).at[:bsz].set(c).at[bsz].set(c_ctx)
    mod = _modulation(cond, w_mod, b_mod)

    def mods(l):
        lat = [mod[l, :bsz, j * D_MODEL:(j + 1) * D_MODEL][:, None, :] for j in range(3)]
        con = [mod[l, bsz:bsz + 1, j * D_MODEL:(j + 1) * D_MODEL][:, None, :] for j in range(3)]
        return lat, con

    (shift, scale, gate), (shift_c, scale_c, gate_c) = mods(0)
    w_in = ab_w_in[0].astype(BF16)
    w_out = ab_w_out[0].astype(BF16)
    sgu_args = (row(sgu_ln_g[0]), row(sgu_ln_b[0]), sgu_w[0].astype(BF16),
                jnp.broadcast_to(sgu_b[0][:, :, None], (SGU_GROUPS, SGU_CHUNK, LANES)))
    lx, ga, yb = _ab_in(x, shift, scale, w_in, *sgu_args, tm=1024)
    lxc, gac, ybc = _ab_in(ctx, shift_c, scale_c, w_in, *sgu_args, tm=nc)
    wpair = _pair_gate_weights(lru_w_a[0], lru_w_x[0])
    ya, yac = _lru(lx, lxc, ga, gac, lru_conv_w[0], row(lru_conv_b[0]), wpair, lru_b_a[0], lru_b_x[0], lru_lambda[0])

    (shift, scale, gate1), (shift_c, scale_c, _) = mods(1)
    w_in = cd_w_in[0].astype(BF16)
    cos, sin_signed = _rope_tables(n)
    x1, z, cg, q, k, v, ag = _ab_out_cd_in(x, ya, yb, gate, w_out, row(ln_g[0]), row(ln_b[0]), shift, scale, w_in,
                                           cos, sin_signed, tm=512)
    kc, vc = _ctx_out_kv(ctx, yac, ybc, gate_c, w_out, row(ln_g[0]), row(ln_b[0]), shift_c, scale_c,
                         w_in[:, 5 * W:7 * W], tm=nc)
    gate = gate1
    w_out = cd_w_out[0].astype(BF16)
    return _attention_out(q, kc, k, vc, v, ag, x1, z, cg, gate, diff_lambda[0], row(diff_subln_g[0]), sconv_w[0], w_out,
                          row(ln_g[1]), row(ln_b[1]), tq=512, ck=(1024, 1024, 256))
```

```python
import functools
import math
from typing import NamedTuple

import jax
import jax.numpy as jnp
from jax import lax
from jax.experimental import pallas as pl
from jax.experimental.pallas import tpu as pltpu

D_MODEL = 1024
DEPTH = 2
GRID_W = 64
W = D_MODEL // 2
LRU_HEAD_DIM = 64
LRU_C = 8.0
LRU_PAIR = 2 * LRU_HEAD_DIM
N_PAIR = W // LRU_PAIR
SGU_CHUNK = 128
SGU_GROUPS = 4
DIFF_V_DIM = 128
DIFF_HEADS = W // DIFF_V_DIM
DIFF_HEAD_DIM = DIFF_V_DIM // 2
ROPE_BASE = 10000.0
DEEPNORM_ALPHA = (2 * DEPTH) ** 0.25
LN_EPS = 1e-5
LAM_INIT_1 = 0.8 - 0.6 * math.exp(-0.3 * 1)
LOG2_E = math.log2(math.e)

SUBLANES = 8
LANES = 128
VMEM_LIMIT = 56 * 1024 * 1024

F32 = jnp.float32
BF16 = jnp.bfloat16

MXU_DIM = 256
PROJ_ROWS = 2 * MXU_DIM
KEY_CHUNK = 4 * MXU_DIM
LRU_ROWS = MXU_DIM
LRU_PAIRS_PER_STEP = 2


class _Tiling(NamedTuple):
    proj_rows: int
    ctx_rows: int
    query_rows: int
    key_chunks: tuple


def _tiling(n, nc):
    assert n % PROJ_ROWS == 0 and n % KEY_CHUNK == 0 and nc % MXU_DIM == 0 and nc <= KEY_CHUNK
    return _Tiling(PROJ_ROWS, nc, PROJ_ROWS, (KEY_CHUNK,) * (n // KEY_CHUNK) + (nc,))


def _silu(x):
    return x * jax.nn.sigmoid(x)


def _dot(a, b):
    return jnp.dot(a, b, preferred_element_type=F32)


def _params(*sem):
    return pltpu.CompilerParams(dimension_semantics=sem, vmem_limit_bytes=VMEM_LIMIT)


def _bcast_spec(arr):
    zeros = (0,) * arr.ndim
    return pl.BlockSpec(arr.shape, lambda *_: zeros)


def _mod_spec(arr):
    if arr.shape[0] == 1:
        return pl.BlockSpec((None, 1, D_MODEL), lambda b, i: (0, 0, 0))
    return pl.BlockSpec((None, 1, D_MODEL), lambda b, i: (b, 0, 0))


def _mod_kernel(cond_ref, w_ref, b_ref, o_ref):
    s = _silu(cond_ref[...]).astype(BF16)
    o_ref[...] = _dot(s, w_ref[...].astype(BF16)) + b_ref[...]


def _modulation(cond, w_mod, b_mod):
    n = cond.shape[0]
    return pl.pallas_call(
        _mod_kernel,
        grid=(DEPTH, 3),
        in_specs=[
            pl.BlockSpec((n, D_MODEL), lambda l, j: (0, 0)),
            pl.BlockSpec((None, D_MODEL, D_MODEL), lambda l, j: (l, 0, j)),
            pl.BlockSpec((None, 1, D_MODEL), lambda l, j: (l, 0, j)),
        ],
        out_specs=pl.BlockSpec((None, n, D_MODEL), lambda l, j: (l, 0, j)),
        out_shape=jax.ShapeDtypeStruct((DEPTH, n, 3 * D_MODEL), F32),
        compiler_params=_params("arbitrary", "arbitrary"),
        name="modulation",
    )(cond, w_mod, b_mod.reshape(DEPTH, 1, 3 * D_MODEL))


def _ab_in_kernel(x_ref, shift_ref, scale_ref, w_ref, lng_ref, lnb_ref, ws_ref, bs_ref,
                  lx_ref, ga_ref, yb_ref, *, tm):
    h = (x_ref[...] * (1.0 + scale_ref[...]) + shift_ref[...]).astype(BF16)

    def proj(j):
        return _dot(h, w_ref[:, j * W:(j + 1) * W])

    lx_ref[...] = proj(0)
    ga_ref[...] = _silu(proj(1)).astype(BF16)
    u = jax.nn.gelu(proj(2))
    v = jax.nn.gelu(proj(3))
    mu = jnp.mean(v, axis=-1, keepdims=True)
    vc = v - mu
    var = jnp.mean(vc * vc, axis=-1, keepdims=True)
    vn = ((vc * lax.rsqrt(var + LN_EPS)) * lng_ref[...] + lnb_ref[...]).astype(BF16)
    ug = u * _silu(proj(4))
    for c in range(tm // SGU_CHUNK):
        rows = slice(c * SGU_CHUNK, (c + 1) * SGU_CHUNK)
        for g in range(SGU_GROUPS):
            cols = slice(g * LANES, (g + 1) * LANES)
            s = _dot(ws_ref[g], vn[rows, cols]) + bs_ref[g]
            yb_ref[rows, cols] = (ug[rows, cols] * s).astype(BF16)


def _ab_in(x, shift, scale, w_in, sgu_ln_g, sgu_ln_b, sgu_w, sgu_b, tm):
    bsz, n, _ = x.shape
    tok = lambda width: pl.BlockSpec((None, tm, width), lambda b, i: (b, i, 0))
    consts = [w_in, sgu_ln_g, sgu_ln_b, sgu_w, sgu_b]
    return pl.pallas_call(
        functools.partial(_ab_in_kernel, tm=tm),
        grid=(bsz, n // tm),
        in_specs=[tok(D_MODEL), _mod_spec(shift), _mod_spec(scale)] + [_bcast_spec(a) for a in consts],
        out_specs=[tok(W), tok(W), tok(W)],
        out_shape=[jax.ShapeDtypeStruct((bsz, n, W), F32),
                   jax.ShapeDtypeStruct((bsz, n, W), BF16),
                   jax.ShapeDtypeStruct((bsz, n, W), BF16)],
        compiler_params=_params("parallel", "parallel"),
        name="ab_in_sgu",
    )(x, shift, scale, *consts)


SCAN_SEGMENTS = SUBLANES
SCAN_BLOCK = 32


def _lru_kernel(lx_ref, lxc_ref, ga_ref, gac_ref, cw_ref, cb_ref, wp_ref, ba_ref, bx_ref, lam_ref,
                ya_ref, yac_ref, xp_ref, xpc_ref, af_ref, bf_ref, ar_ref, br_ref, hf_ref, hr_ref, *, n, nc, tr):
    nt = n + nc
    lseg = nt // SCAN_SEGMENTS
    pad = SUBLANES
    wl = lx_ref.shape[-1]
    npair = wl // LANES
    zeros = jnp.zeros((pad, wl), F32)
    for ref, src, length in ((xp_ref, lx_ref, n), (xpc_ref, lxc_ref, nc)):
        ref[0:pad, :] = zeros
        ref[pad + length:2 * pad + length, :] = zeros
        ref[pad:pad + length, :] = src[...]

    lam = lam_ref[...]
    neg = -lam
    softplus = jnp.maximum(neg, 0.0) + jnp.log1p(jnp.exp(-jnp.abs(neg)))
    rate = LRU_C * softplus
    rate_log2 = -LOG2_E * rate
    cw = cw_ref[...]
    cb = cb_ref[...]

    def block_rows(pos):
        if isinstance(pos, int):
            seg, step = divmod(pos, lseg)
        else:
            seg = lax.div(pos, lseg)
            step = pos - seg * lseg
        return pl.ds(step * SCAN_SEGMENTS + seg, SCAN_BLOCK, stride=SCAN_SEGMENTS)

    def coeffs(src_ref, r0, f_off, r_off):
        xw = src_ref[pl.ds(r0, tr + 2 * pad), :]
        xc = cb + sum(cw[k:k + 1, :] * xw[pad - 2 + k:pad - 2 + k + tr, :] for k in range(4))
        xcb = xc.astype(BF16)
        dst = [[block_rows(off + r0 + blk) for blk in range(0, tr, SCAN_BLOCK)] for off in (f_off, r_off)]
        for j in range(npair):
            cols = slice(j * LANES, (j + 1) * LANES)
            gts = _dot(xcb[:, cols], wp_ref[j])
            xj = xc[:, cols]
            for d, (a_ref, b_ref) in enumerate(((af_ref, bf_ref), (ar_ref, br_ref))):
                gate_r = jax.nn.sigmoid(gts[:, (2 * d) * LANES:(2 * d + 1) * LANES] + ba_ref[d:d + 1, cols])
                gate_i = jax.nn.sigmoid(gts[:, (2 * d + 1) * LANES:(2 * d + 2) * LANES] + bx_ref[d:d + 1, cols])
                a = jnp.exp2(gate_r * rate_log2[d:d + 1, cols])
                one_minus_a2 = jnp.tanh(gate_r * rate[d:d + 1, cols]) * (1.0 + a * a)
                root = jnp.where(one_minus_a2 > 0.0, one_minus_a2 * lax.rsqrt(one_minus_a2), 0.0)
                b = root * (gate_i * xj)
                for rows, blk in zip(dst[d], range(0, tr, SCAN_BLOCK)):
                    a_ref[j, rows, :] = a[blk:blk + SCAN_BLOCK, :]
                    b_ref[j, rows, :] = b[blk:blk + SCAN_BLOCK, :]

    def lat_tile(i, carry):
        coeffs(xp_ref, pl.multiple_of(i * tr, tr), nc, 0)
        return carry

    lax.fori_loop(0, n // tr, lat_tile, 0)
    for i in range(nc // tr):
        coeffs(xpc_ref, i * tr, 0, n)

    chains = [(a_ref, b_ref, h_ref, j, rev)
              for a_ref, b_ref, h_ref, rev in ((af_ref, bf_ref, hf_ref, False), (ar_ref, br_ref, hr_ref, True))
              for j in range(npair)]

    def step_rows(t, rev):
        return pl.ds(pl.multiple_of((lseg - 1 - t if rev else t) * SCAN_SEGMENTS, SCAN_SEGMENTS), SCAN_SEGMENTS)

    def totals_step(t, carry):
        out = []
        for (a_ref, b_ref, _, j, rev), (decay, h) in zip(chains, carry):
            rows = step_rows(t, rev)
            a = a_ref[j, rows, :]
            out.append((a * decay, a * h + b_ref[j, rows, :]))
        return tuple(out)

    one = jnp.ones((SCAN_SEGMENTS, LANES), F32)
    zero = jnp.zeros((SCAN_SEGMENTS, LANES), F32)
    totals = lax.fori_loop(0, lseg, totals_step, tuple((one, zero) for _ in chains), unroll=8)

    def entering_state(decay, h_end, rev):
        order = range(SCAN_SEGMENTS - 1, -1, -1) if rev else range(SCAN_SEGMENTS)
        rows = [None] * SCAN_SEGMENTS
        state = jnp.zeros((1, LANES), F32)
        for s in order:
            rows[s] = state
            state = decay[s:s + 1, :] * state + h_end[s:s + 1, :]
        return jnp.concatenate(rows, axis=0)

    starts = tuple(entering_state(decay, h_end, chain[-1]) for chain, (decay, h_end) in zip(chains, totals))

    def states_step(t, carry):
        out = []
        for (a_ref, b_ref, h_ref, j, rev), h in zip(chains, carry):
            rows = step_rows(t, rev)
            h = a_ref[j, rows, :] * h + b_ref[j, rows, :]
            h_ref[j, rows, :] = h
            out.append(h)
        return tuple(out)

    lax.fori_loop(0, lseg, states_step, starts, unroll=8)

    def write_out(y_ref, g_ref, r0, f_pos, r_pos, rows):
        for blk in range(0, rows, SCAN_BLOCK):
            dst = pl.ds(r0 + blk, SCAN_BLOCK)
            rows_f = block_rows(f_pos + blk)
            rows_r = block_rows(r_pos + blk)
            for j in range(npair):
                cols = slice(j * LANES, (j + 1) * LANES)
                h = hf_ref[j, rows_f, :] + hr_ref[j, rows_r, :]
                y_ref[dst, cols] = (h * g_ref[dst, cols].astype(F32)).astype(BF16)

    def out_tile(i, carry):
        r0 = pl.multiple_of(i * tr, tr)
        write_out(ya_ref, ga_ref, r0, nc + r0, r0, tr)
        return carry

    lax.fori_loop(0, n // tr, out_tile, 0)
    write_out(yac_ref, gac_ref, 0, 0, n, nc)


def _lru(lx, lxc, ga, gac, conv_w, conv_b, wpair, b_a, b_x, lam):
    bsz, n, _ = lx.shape
    nc = lxc.shape[1]
    tr = LRU_ROWS
    lseg, rem = divmod(n + nc, SCAN_SEGMENTS)
    assert rem == 0 and lseg % SCAN_BLOCK == 0 and n % tr == 0 and nc % tr == 0 and tr % SCAN_BLOCK == 0
    npair = LRU_PAIRS_PER_STEP
    wl = npair * LANES
    seq = lambda length: pl.BlockSpec((None, length, wl), lambda b, g: (b, 0, g))
    vec = lambda rows: pl.BlockSpec((rows, wl), lambda b, g: (0, g))
    pad = 2 * SUBLANES
    coeff = pltpu.VMEM((npair, SCAN_SEGMENTS * lseg, LANES), F32)
    return pl.pallas_call(
        functools.partial(_lru_kernel, n=n, nc=nc, tr=tr),
        grid=(bsz, N_PAIR // npair),
        in_specs=[seq(n), seq(nc), seq(n), seq(nc), vec(conv_w.shape[0]), vec(1),
                  pl.BlockSpec((npair,) + wpair.shape[1:], lambda b, g: (g, 0, 0)), vec(2), vec(2), vec(2)],
        out_specs=[seq(n), seq(nc)],
        out_shape=[jax.ShapeDtypeStruct((bsz, n, W), BF16), jax.ShapeDtypeStruct((bsz, nc, W), BF16)],
        scratch_shapes=[pltpu.VMEM((n + pad, wl), F32), pltpu.VMEM((nc + pad, wl), F32)] + [coeff] * 6,
        compiler_params=_params("parallel", "parallel"),
        name="rglru",
    )(lx, lxc, ga, gac, conv_w, conv_b, wpair, b_a, b_x, lam)


def _residual_ln(x, y, gate, g, b):
    z = DEEPNORM_ALPHA * x + gate * y
    mu = jnp.mean(z, axis=-1, keepdims=True)
    zc = z - mu
    var = jnp.mean(zc * zc, axis=-1, keepdims=True)
    return (zc * lax.rsqrt(var + LN_EPS)) * g + b


def _ctx_out_kv_kernel(x_ref, ya_ref, yb_ref, gate_ref, wo_ref, g_ref, b_ref, shift_ref, scale_ref, w_ref,
                       k_ref, v_ref):
    y = _dot(ya_ref[...], wo_ref[0:W, :]) + _dot(yb_ref[...], wo_ref[W:2 * W, :])
    c1 = _residual_ln(x_ref[...], y, gate_ref[...], g_ref[...], b_ref[...])
    h = (c1 * (1.0 + scale_ref[...]) + shift_ref[...]).astype(BF16)
    k_ref[...] = _dot(h, w_ref[:, 0:W]).astype(BF16)
    v_ref[...] = _dot(h, w_ref[:, W:2 * W]).astype(BF16)


def _ctx_out_kv(ctx, ya, yb, gate0, w_out, ln_g, ln_b, shift, scale, w_kv, tm):
    bsz, n, _ = ctx.shape
    tok = lambda width: pl.BlockSpec((None, tm, width), lambda b, i: (b, i, 0))
    return pl.pallas_call(
        _ctx_out_kv_kernel,
        grid=(bsz, n // tm),
        in_specs=[tok(D_MODEL), tok(W), tok(W), _mod_spec(gate0), _bcast_spec(w_out), _bcast_spec(ln_g),
                  _bcast_spec(ln_b), _mod_spec(shift), _mod_spec(scale), _bcast_spec(w_kv)],
        out_specs=[tok(W), tok(W)],
        out_shape=[jax.ShapeDtypeStruct((bsz, n, W), BF16)] * 2,
        compiler_params=_params("parallel", "parallel"),
        name="ctx_out_kv",
    )(ctx, ya, yb, gate0, w_out, ln_g, ln_b, shift, scale, w_kv)


def _rope(t, cos, sin_signed, first_half):
    parts = []
    for j in range(W // LANES):
        tj = t[:, j * LANES:(j + 1) * LANES]
        parts.append(jnp.where(first_half, pltpu.roll(tj, LANES - 16, 1), pltpu.roll(tj, 16, 1)))
    return t * cos + jnp.concatenate(parts, axis=1) * sin_signed


def _ab_out_cd_in_kernel(x_ref, ya_ref, yb_ref, gate_ref, wo_ref, g_ref, b_ref, shift_ref, scale_ref, w_ref,
                         cos_ref, sin_ref, x1_ref, z_ref, cg_ref, q_ref, k_ref, v_ref, ag_ref, *, tm):
    y = _dot(ya_ref[...], wo_ref[0:W, :]) + _dot(yb_ref[...], wo_ref[W:2 * W, :])
    x1 = _residual_ln(x_ref[...], y, gate_ref[...], g_ref[...], b_ref[...])
    x1_ref[...] = x1
    h = (x1 * (1.0 + scale_ref[...]) + shift_ref[...]).astype(BF16)

    def proj(j):
        return _dot(h, w_ref[:, j * W:(j + 1) * W])

    cos = cos_ref[...]
    sin = sin_ref[...]
    lane = lax.broadcasted_iota(jnp.int32, (tm, LANES), 1)
    first_half = (lane % 32) < 16
    q_ref[...] = (_rope(proj(4), cos, sin, first_half) * (DIFF_HEAD_DIM ** -0.5 * LOG2_E)).astype(BF16)
    k_ref[...] = _rope(proj(5), cos, sin, first_half).astype(BF16)
    cg_ref[...] = (proj(1) * _silu(proj(3))).astype(BF16)
    ag_ref[...] = _silu(proj(7)).astype(BF16)
    z_ref[...] = proj(2) * proj(0)
    v_ref[...] = proj(6).astype(BF16)


def _ab_out_cd_in(x, ya, yb, gate0, w_out, ln_g, ln_b, shift, scale, w_in, cos, sin_signed, tm):
    bsz, n, _ = x.shape
    tok = lambda width: pl.BlockSpec((None, tm, width), lambda i, b: (b, i, 0))
    mod = pl.BlockSpec((None, 1, D_MODEL), lambda i, b: (b, 0, 0))
    table = pl.BlockSpec((tm, W), lambda i, b: (i, 0))
    const = lambda a: pl.BlockSpec(a.shape, lambda i, b: (0,) * a.ndim, pipeline_mode=pl.Buffered(1))
    return pl.pallas_call(
        functools.partial(_ab_out_cd_in_kernel, tm=tm),
        grid=(n // tm, bsz),
        in_specs=[tok(D_MODEL), tok(W), tok(W), mod, const(w_out), const(ln_g), const(ln_b), mod, mod, const(w_in),
                  table, table],
        out_specs=[tok(D_MODEL)] + [tok(W)] * 6,
        out_shape=[jax.ShapeDtypeStruct((bsz, n, D_MODEL), F32), jax.ShapeDtypeStruct((bsz, n, W), F32)]
        + [jax.ShapeDtypeStruct((bsz, n, W), BF16)] * 5,
        compiler_params=_params("parallel", "parallel"),
        name="ab_out_cd_in",
    )(x, ya, yb, gate0, w_out, ln_g, ln_b, shift, scale, w_in, cos, sin_signed)


def _attn_out_kernel(q_ref, kc_ref, k_ref, vc_ref, v_ref, ag_ref, x_ref, z_ref, zprev_ref, znext_ref, cg_ref,
                     gate_ref, dl_ref, sg_ref, cw_ref, w_ref, g_ref, b_ref, o_ref,
                     kall_ref, vall_ref, qh_ref, yd_ref, s_ref, p_ref, al_ref, m_ref, acc_ref, *, tq, ck):
    nc = kc_ref.shape[0]
    n = k_ref.shape[0]
    nck = len(ck)
    starts = [sum(ck[:j]) for j in range(nck)]
    i = pl.program_id(1)

    @pl.when(i == 0)
    def _():
        for h in range(DIFF_HEADS):
            cols = slice(h * DIFF_V_DIM, (h + 1) * DIFF_V_DIM)
            kall_ref[h, 0:n, :] = k_ref[:, cols]
            kall_ref[h, n:n + nc, :] = kc_ref[:, cols]
            vall_ref[h, 0:n, 0:DIFF_V_DIM] = v_ref[:, cols]
            vall_ref[h, n:n + nc, 0:DIFF_V_DIM] = vc_ref[:, cols]
            vall_ref[h, :, DIFF_V_DIM:2 * DIFF_V_DIM] = jnp.ones((n + nc, DIFF_V_DIM), BF16)

    dl = dl_ref[...]
    lam = (jnp.exp(jnp.sum(dl[0:1, :] * dl[1:2, :], axis=-1, keepdims=True))
           - jnp.exp(jnp.sum(dl[2:3, :] * dl[3:4, :], axis=-1, keepdims=True)) + LAM_INIT_1)
    lane = lax.broadcasted_iota(jnp.int32, (tq, DIFF_V_DIM), 1)
    for h in range(DIFF_HEADS):
        q = q_ref[:, h * DIFF_V_DIM:(h + 1) * DIFF_V_DIM]
        zero = jnp.zeros_like(q)
        qh_ref[h, 0:tq, :] = jnp.where(lane < DIFF_HEAD_DIM, q, zero)
        qh_ref[h, tq:2 * tq, :] = jnp.where(lane >= DIFF_HEAD_DIM, q, zero)
    nt = (((1,), (1,)), ((), ()))
    slots = s_ref.shape[0]

    def head(h, carry):
        qq = qh_ref[h]

        def scores(j, slot):
            kj = kall_ref[h, starts[j]:starts[j] + ck[j], :]
            s_ref[slot, :, 0:ck[j]] = lax.dot_general(qq, kj, nt, preferred_element_type=F32)

        def softmax(j, slot, first):
            s = s_ref[slot, :, 0:ck[j]]
            sm = s[:, 0:LANES]
            for c in range(1, ck[j] // LANES):
                sm = jnp.maximum(sm, s[:, c * LANES:(c + 1) * LANES])
            m_new = jnp.broadcast_to(jnp.max(sm, axis=-1, keepdims=True), (2 * tq, LANES))
            if not first:
                m_old = m_ref[...]
                m_new = jnp.maximum(m_old, m_new)
                al_ref[slot] = jnp.exp2(m_old - m_new)
            m_ref[...] = m_new
            p_ref[slot, :, 0:ck[j]] = jnp.exp2(s - jnp.concatenate([m_new] * (ck[j] // LANES), axis=1)).astype(BF16)

        def weighted_values(j, slot, first):
            pv = _dot(p_ref[slot, :, 0:ck[j]], vall_ref[h, starts[j]:starts[j] + ck[j], :])
            if first:
                acc_ref[...] = pv
            else:
                alpha = al_ref[slot]
                acc_ref[...] = jnp.concatenate([alpha, alpha], axis=1) * acc_ref[...] + pv

        for t in range(nck + 2):
            if t < nck:
                scores(t, t % slots)
            if 1 <= t <= nck:
                softmax(t - 1, (t - 1) % slots, t == 1)
            if t >= 2:
                weighted_values(t - 2, (t - 2) % slots, t == 2)

        acc = acc_ref[:, 0:DIFF_V_DIM]
        w = 1.0 / acc_ref[:, DIFF_V_DIM:2 * DIFF_V_DIM]
        o = acc[0:tq, :] * w[0:tq, :] - acc[tq:2 * tq, :] * (lam * w[tq:2 * tq, :])
        yd_ref[h] = (o * lax.rsqrt(jnp.mean(o * o, axis=-1, keepdims=True) + LN_EPS)) * sg_ref[...] * (1.0 - LAM_INIT_1)
        return carry

    lax.fori_loop(0, DIFF_HEADS, head, 0, unroll=True)

    yd = jnp.concatenate([yd_ref[h] for h in range(DIFF_HEADS)], axis=1) * ag_ref[...].astype(F32)
    z = z_ref[...]
    row = lax.broadcasted_iota(jnp.int32, (tq, W), 0)
    prev_row = jnp.where(i > 0, zprev_ref[SUBLANES - 1:SUBLANES, :], 0.0)
    next_row = jnp.where(i < pl.num_programs(1) - 1, znext_ref[0:1, :], 0.0)
    z_m1 = jnp.where(row == 0, prev_row, pltpu.roll(z, 1, 0))
    z_p1 = jnp.where(row == tq - 1, next_row, pltpu.roll(z, tq - 1, 0))
    cw = cw_ref[...]
    conv = cw[0:1, :] * z_m1 + cw[1:2, :] * z + cw[2:3, :] * z_p1
    yc = (cg_ref[...].astype(F32) * conv).astype(BF16)
    y = _dot(yc, w_ref[0:W, :]) + _dot(yd.astype(BF16), w_ref[W:2 * W, :])
    o_ref[...] = _residual_ln(x_ref[...], y, gate_ref[...], g_ref[...], b_ref[...])


def _attention_out(q, kc, k, vc, v, ag, x, z, cg, gate, diff_lambda, subln_g, sconv_w, w_out, ln_g, ln_b, tq, ck):
    bsz, n, _ = q.shape
    nc = kc.shape[1]
    assert sum(ck) == n + nc and all(c % (2 * LANES) == 0 for c in ck)
    tok = lambda width: pl.BlockSpec((None, tq, width), lambda b, i: (b, i, 0))
    keys = lambda length: pl.BlockSpec((None, length, W), lambda b, i: (b, 0, 0))
    per = tq // SUBLANES
    last = n // SUBLANES - 1
    zprev = pl.BlockSpec((None, SUBLANES, W), lambda b, i: (b, jnp.maximum(i * per - 1, 0), 0))
    znext = pl.BlockSpec((None, SUBLANES, W), lambda b, i: (b, jnp.minimum((i + 1) * per, last), 0))
    const = lambda a: pl.BlockSpec(a.shape, lambda b, i: (0,) * a.ndim, pipeline_mode=pl.Buffered(1))
    consts = [diff_lambda, subln_g, sconv_w, w_out, ln_g, ln_b]
    slots = 2
    return pl.pallas_call(
        functools.partial(_attn_out_kernel, tq=tq, ck=ck),
        grid=(bsz, n // tq),
        in_specs=[tok(W), keys(nc), keys(n), keys(nc), keys(n), tok(W), tok(D_MODEL), tok(W), zprev, znext, tok(W),
                  _mod_spec(gate)] + [const(a) for a in consts],
        out_specs=tok(D_MODEL),
        out_shape=jax.ShapeDtypeStruct((bsz, n, D_MODEL), F32),
        scratch_shapes=[pltpu.VMEM((DIFF_HEADS, n + nc, DIFF_V_DIM), BF16),
                        pltpu.VMEM((DIFF_HEADS, n + nc, 2 * DIFF_V_DIM), BF16),
                        pltpu.VMEM((DIFF_HEADS, 2 * tq, DIFF_V_DIM), BF16),
                        pltpu.VMEM((DIFF_HEADS, tq, DIFF_V_DIM), F32),
                        pltpu.VMEM((slots, 2 * tq, max(ck)), F32), pltpu.VMEM((slots, 2 * tq, max(ck)), BF16),
                        pltpu.VMEM((slots, 2 * tq, LANES), F32), pltpu.VMEM((2 * tq, LANES), F32),
                        pltpu.VMEM((2 * tq, 2 * DIFF_V_DIM), F32)],
        compiler_params=_params("parallel", "arbitrary"),
        name="attention_out",
    )(q, kc, k, vc, v, ag, x, z, z, z, cg, gate, *consts)


def _pair_gate_weights(w_a, w_x):
    def pair_blockdiag(w):
        wp = w.reshape(N_PAIR, 2, LRU_HEAD_DIM, LRU_HEAD_DIM)
        zero = jnp.zeros_like(wp[:, 0])
        top = jnp.concatenate([wp[:, 0], zero], axis=-1)
        bot = jnp.concatenate([zero, wp[:, 1]], axis=-1)
        return jnp.concatenate([top, bot], axis=-2)
    return jnp.concatenate([pair_blockdiag(w_a[0]), pair_blockdiag(w_x[0]),
                            pair_blockdiag(w_a[1]), pair_blockdiag(w_x[1])], axis=-1).astype(BF16)


def _rope_tables(n):
    rows = n // GRID_W
    row = jnp.repeat(jnp.arange(rows, dtype=F32), GRID_W)
    col = jnp.tile(jnp.arange(GRID_W, dtype=F32), rows)
    n_freq = DIFF_HEAD_DIM // 4
    inv = ROPE_BASE ** (-jnp.arange(n_freq, dtype=F32) / n_freq)
    ang_r = row[:, None] * inv
    ang_c = col[:, None] * inv
    ang = jnp.concatenate([ang_r, ang_r, ang_c, ang_c], axis=-1)
    sign = jnp.tile(jnp.concatenate([-jnp.ones((n_freq,), F32), jnp.ones((n_freq,), F32)]), 2)
    reps = W // DIFF_HEAD_DIM
    return jnp.tile(jnp.cos(ang), (1, reps)), jnp.tile(jnp.sin(ang) * sign, (1, reps))


def kernel(x, c, ctx, c_ctx, w_mod, b_mod, ln_g, ln_b, ab_w_in, ab_w_out, lru_conv_w, lru_conv_b, lru_w_a, lru_b_a,
           lru_w_x, lru_b_x, lru_lambda, sgu_ln_g, sgu_ln_b, sgu_w, sgu_b, cd_w_in, cd_w_out, sconv_w, diff_lambda,
           diff_subln_g):
    bsz, n, _ = x.shape
    nc = ctx.shape[1]
    tiles = _tiling(n, nc)
    row = lambda a: a.reshape(1, -1)

    n_cond = -(-(bsz + 1) // SUBLANES) * SUBLANES
    cond = jnp.zeros((n_cond, D_MODEL), F32).at[:bsz].set(c).at[bsz].set(c_ctx)
    mod = _modulation(cond, w_mod, b_mod)

    def mods(l):
        lat = [mod[l, :bsz, j * D_MODEL:(j + 1) * D_MODEL][:, None, :] for j in range(3)]
        con = [mod[l, bsz:bsz + 1, j * D_MODEL:(j + 1) * D_MODEL][:, None, :] for j in range(3)]
        return lat, con

    (shift, scale, gate), (shift_c, scale_c, gate_c) = mods(0)
    w_in = ab_w_in[0].astype(BF16)
    w_out = ab_w_out[0].astype(BF16)
    sgu_args = (row(sgu_ln_g[0]), row(sgu_ln_b[0]), sgu_w[0].astype(BF16),
                jnp.broadcast_to(sgu_b[0][:, :, None], (SGU_GROUPS, SGU_CHUNK, LANES)))
    lx, ga, yb = _ab_in(x, shift, scale, w_in, *sgu_args, tm=tiles.proj_rows)
    lxc, gac, ybc = _ab_in(ctx, shift_c, scale_c, w_in, *sgu_args, tm=tiles.ctx_rows)
    wpair = _pair_gate_weights(lru_w_a[0], lru_w_x[0])
    ya, yac = _lru(lx, lxc, ga, gac, lru_conv_w[0], row(lru_conv_b[0]), wpair, lru_b_a[0], lru_b_x[0], lru_lambda[0])

    (shift, scale, gate1), (shift_c, scale_c, _) = mods(1)
    w_in = cd_w_in[0].astype(BF16)
    cos, sin_signed = _rope_tables(n)
    x1, z, cg, q, k, v, ag = _ab_out_cd_in(x, ya, yb, gate, w_out, row(ln_g[0]), row(ln_b[0]), shift, scale, w_in,
                                           cos, sin_signed, tm=tiles.proj_rows)
    kc, vc = _ctx_out_kv(ctx, yac, ybc, gate_c, w_out, row(ln_g[0]), row(ln_b[0]), shift_c, scale_c,
                         w_in[:, 5 * W:7 * W], tm=tiles.ctx_rows)
    gate = gate1
    w_out = cd_w_out[0].astype(BF16)
    return _attention_out(q, kc, k, vc, v, ag, x1, z, cg, gate, diff_lambda[0], row(diff_subln_g[0]), sconv_w[0], w_out,
                          row(ln_g[1]), row(ln_b[1]), tq=tiles.query_rows, ck=tiles.key_chunks)
```

```python
import functools
import math
from typing import NamedTuple

import jax
import jax.numpy as jnp
from jax import lax
from jax.experimental import pallas as pl
from jax.experimental.pallas import tpu as pltpu

D_MODEL = 1024
DEPTH = 2
GRID_W = 64
W = D_MODEL // 2
LRU_HEAD_DIM = 64
LRU_C = 8.0
LRU_PAIR = 2 * LRU_HEAD_DIM
N_PAIR = W // LRU_PAIR
SGU_CHUNK = 128
SGU_GROUPS = 4
DIFF_V_DIM = 128
DIFF_HEADS = W // DIFF_V_DIM
DIFF_HEAD_DIM = DIFF_V_DIM // 2
ROPE_BASE = 10000.0
DEEPNORM_ALPHA = (2 * DEPTH) ** 0.25
LN_EPS = 1e-5
LAM_INIT_1 = 0.8 - 0.6 * math.exp(-0.3 * 1)
LOG2_E = math.log2(math.e)

SUBLANES = 8
LANES = 128
VMEM_LIMIT = 56 * 1024 * 1024

F32 = jnp.float32
BF16 = jnp.bfloat16

MXU_DIM = 256
PROJ_ROWS = 2 * MXU_DIM
KEY_CHUNK = 4 * MXU_DIM
LRU_ROWS = MXU_DIM
LRU_PAIRS_PER_STEP = 2


class _Tiling(NamedTuple):
    proj_rows: int
    ctx_rows: int
    query_rows: int
    key_chunks: tuple


def _tiling(n, nc):
    assert n % PROJ_ROWS == 0 and n % KEY_CHUNK == 0 and nc % MXU_DIM == 0 and nc <= KEY_CHUNK
    return _Tiling(PROJ_ROWS, nc, PROJ_ROWS, (KEY_CHUNK,) * (n // KEY_CHUNK) + (nc,))


def _silu(x):
    return x * jax.nn.sigmoid(x)


def _dot(a, b):
    return jnp.dot(a, b, preferred_element_type=F32)


def _params(*sem):
    return pltpu.CompilerParams(dimension_semantics=sem, vmem_limit_bytes=VMEM_LIMIT)


def _bcast_spec(arr):
    zeros = (0,) * arr.ndim
    return pl.BlockSpec(arr.shape, lambda *_: zeros)


def _mod_spec(arr):
    if arr.shape[0] == 1:
        return pl.BlockSpec((None, 1, D_MODEL), lambda b, i: (0, 0, 0))
    return pl.BlockSpec((None, 1, D_MODEL), lambda b, i: (b, 0, 0))


def _mod_kernel(cond_ref, w_ref, b_ref, o_ref):
    s = _silu(cond_ref[...]).astype(BF16)
    o_ref[...] = _dot(s, w_ref[...].astype(BF16)) + b_ref[...]


def _modulation(cond, w_mod, b_mod):
    n = cond.shape[0]
    return pl.pallas_call(
        _mod_kernel,
        grid=(DEPTH, 3),
        in_specs=[
            pl.BlockSpec((n, D_MODEL), lambda l, j: (0, 0)),
            pl.BlockSpec((None, D_MODEL, D_MODEL), lambda l, j: (l, 0, j)),
            pl.BlockSpec((None, 1, D_MODEL), lambda l, j: (l, 0, j)),
        ],
        out_specs=pl.BlockSpec((None, n, D_MODEL), lambda l, j: (l, 0, j)),
        out_shape=jax.ShapeDtypeStruct((DEPTH, n, 3 * D_MODEL), F32),
        compiler_params=_params("arbitrary", "arbitrary"),
        name="modulation",
    )(cond, w_mod, b_mod.reshape(DEPTH, 1, 3 * D_MODEL))


def _ab_in_kernel(x_ref, shift_ref, scale_ref, w_ref, lng_ref, lnb_ref, ws_ref, bs_ref,
                  lx_ref, ga_ref, yb_ref, *, tm):
    h = (x_ref[...] * (1.0 + scale_ref[...]) + shift_ref[...]).astype(BF16)

    def proj(j):
        return _dot(h, w_ref[:, j * W:(j + 1) * W])

    lx_ref[...] = proj(0)
    ga_ref[...] = _silu(proj(1)).astype(BF16)
    u = jax.nn.gelu(proj(2))
    v = jax.nn.gelu(proj(3))
    mu = jnp.mean(v, axis=-1, keepdims=True)
    vc = v - mu
    var = jnp.mean(vc * vc, axis=-1, keepdims=True)
    vn = ((vc * lax.rsqrt(var + LN_EPS)) * lng_ref[...] + lnb_ref[...]).astype(BF16)
    ug = u * _silu(proj(4))
    for c in range(tm // SGU_CHUNK):
        rows = slice(c * SGU_CHUNK, (c + 1) * SGU_CHUNK)
        for g in range(SGU_GROUPS):
            cols = slice(g * LANES, (g + 1) * LANES)
            s = _dot(ws_ref[g], vn[rows, cols]) + bs_ref[g]
            yb_ref[rows, cols] = (ug[rows, cols] * s).astype(BF16)


def _ab_in(x, shift, scale, w_in, sgu_ln_g, sgu_ln_b, sgu_w, sgu_b, tm):
    bsz, n, _ = x.shape
    tok = lambda width: pl.BlockSpec((None, tm, width), lambda b, i: (b, i, 0))
    consts = [w_in, sgu_ln_g, sgu_ln_b, sgu_w, sgu_b]
    return pl.pallas_call(
        functools.partial(_ab_in_kernel, tm=tm),
        grid=(bsz, n // tm),
        in_specs=[tok(D_MODEL), _mod_spec(shift), _mod_spec(scale)] + [_bcast_spec(a) for a in consts],
        out_specs=[tok(W), tok(W), tok(W)],
        out_shape=[jax.ShapeDtypeStruct((bsz, n, W), F32),
                   jax.ShapeDtypeStruct((bsz, n, W), BF16),
                   jax.ShapeDtypeStruct((bsz, n, W), BF16)],
        compiler_params=_params("parallel", "parallel"),
        name="ab_in_sgu",
    )(x, shift, scale, *consts)


SCAN_SEGMENTS = SUBLANES
SCAN_BLOCK = 32
SCAN_STATE_PAD = SUBLANES


def _lru_kernel(lx_ref, lxc_ref, ga_ref, gac_ref, cw_ref, cb_ref, wp_ref, ba_ref, bx_ref, lam_ref,
                ya_ref, yac_ref, xp_ref, xpc_ref, af_ref, bf_ref, ar_ref, br_ref, hf_ref, hr_ref, *, n, nc, tr):
    nt = n + nc
    lseg = nt // SCAN_SEGMENTS
    h_pitch = lseg + SCAN_STATE_PAD
    pad = SUBLANES
    wl = lx_ref.shape[-1]
    npair = wl // LANES
    zeros = jnp.zeros((pad, wl), F32)
    for ref, src, length in ((xp_ref, lx_ref, n), (xpc_ref, lxc_ref, nc)):
        ref[0:pad, :] = zeros
        ref[pad + length:2 * pad + length, :] = zeros
        ref[pad:pad + length, :] = src[...]

    lam = lam_ref[...]
    neg = -lam
    softplus = jnp.maximum(neg, 0.0) + jnp.log1p(jnp.exp(-jnp.abs(neg)))
    rate = LRU_C * softplus
    rate_log2 = -LOG2_E * rate
    cw = cw_ref[...]
    cb = cb_ref[...]

    def block_rows(pos):
        if isinstance(pos, int):
            seg, step = divmod(pos, lseg)
        else:
            seg = lax.div(pos, lseg)
            step = pos - seg * lseg
        return pl.ds(step * SCAN_SEGMENTS + seg, SCAN_BLOCK, stride=SCAN_SEGMENTS)

    def coeffs(src_ref, r0, f_off, r_off):
        xw = src_ref[pl.ds(r0, tr + 2 * pad), :]
        xc = cb + sum(cw[k:k + 1, :] * xw[pad - 2 + k:pad - 2 + k + tr, :] for k in range(4))
        xcb = xc.astype(BF16)
        dst = [[block_rows(off + r0 + blk) for blk in range(0, tr, SCAN_BLOCK)] for off in (f_off, r_off)]
        for j in range(npair):
            cols = slice(j * LANES, (j + 1) * LANES)
            gts = _dot(xcb[:, cols], wp_ref[j])
            xj = xc[:, cols]
            for d, (a_ref, b_ref) in enumerate(((af_ref, bf_ref), (ar_ref, br_ref))):
                gate_r = jax.nn.sigmoid(gts[:, (2 * d) * LANES:(2 * d + 1) * LANES] + ba_ref[d:d + 1, cols])
                gate_i = jax.nn.sigmoid(gts[:, (2 * d + 1) * LANES:(2 * d + 2) * LANES] + bx_ref[d:d + 1, cols])
                a = jnp.exp2(gate_r * rate_log2[d:d + 1, cols])
                one_minus_a2 = jnp.tanh(gate_r * rate[d:d + 1, cols]) * (1.0 + a * a)
                root = jnp.where(one_minus_a2 > 0.0, one_minus_a2 * lax.rsqrt(one_minus_a2), 0.0)
                b = root * (gate_i * xj)
                for rows, blk in zip(dst[d], range(0, tr, SCAN_BLOCK)):
                    a_ref[j, rows, :] = a[blk:blk + SCAN_BLOCK, :]
                    b_ref[j, rows, :] = b[blk:blk + SCAN_BLOCK, :]

    def lat_tile(i, carry):
        coeffs(xp_ref, pl.multiple_of(i * tr, tr), nc, 0)
        return carry

    lax.fori_loop(0, n // tr, lat_tile, 0)
    for i in range(nc // tr):
        coeffs(xpc_ref, i * tr, 0, n)

    chains = [(a_ref, b_ref, h_ref, j, rev)
              for a_ref, b_ref, h_ref, rev in ((af_ref, bf_ref, hf_ref, False), (ar_ref, br_ref, hr_ref, True))
              for j in range(npair)]

    def step_rows(t, rev):
        return pl.ds(pl.multiple_of((lseg - 1 - t if rev else t) * SCAN_SEGMENTS, SCAN_SEGMENTS), SCAN_SEGMENTS)

    def totals_step(t, carry):
        out = []
        for (a_ref, b_ref, _, j, rev), (decay, h) in zip(chains, carry):
            rows = step_rows(t, rev)
            a = a_ref[j, rows, :]
            out.append((a * decay, a * h + b_ref[j, rows, :]))
        return tuple(out)

    one = jnp.ones((SCAN_SEGMENTS, LANES), F32)
    zero = jnp.zeros((SCAN_SEGMENTS, LANES), F32)
    totals = lax.fori_loop(0, lseg, totals_step, tuple((one, zero) for _ in chains), unroll=8)

    def entering_state(decay, h_end, rev):
        order = range(SCAN_SEGMENTS - 1, -1, -1) if rev else range(SCAN_SEGMENTS)
        rows = [None] * SCAN_SEGMENTS
        state = jnp.zeros((1, LANES), F32)
        for s in order:
            rows[s] = state
            state = decay[s:s + 1, :] * state + h_end[s:s + 1, :]
        return jnp.concatenate(rows, axis=0)

    starts = tuple(entering_state(decay, h_end, chain[-1]) for chain, (decay, h_end) in zip(chains, totals))

    def states_step(t, carry):
        out = []
        for (a_ref, b_ref, h_ref, j, rev), h in zip(chains, carry):
            rows = step_rows(t, rev)
            h = a_ref[j, rows, :] * h + b_ref[j, rows, :]
            h_ref[j, pl.ds(lseg - 1 - t if rev else t, SCAN_SEGMENTS, stride=h_pitch), :] = h
            out.append(h)
        return tuple(out)

    lax.fori_loop(0, lseg, states_step, starts, unroll=8)

    def state_rows(pos):
        if isinstance(pos, int):
            seg, step = divmod(pos, lseg)
            return pl.ds(seg * h_pitch + step, SCAN_BLOCK)
        seg = lax.div(pos, lseg)
        return pl.ds(pl.multiple_of(seg * h_pitch + (pos - seg * lseg), SUBLANES), SCAN_BLOCK)

    def write_out(y_ref, g_ref, r0, f_pos, r_pos, rows):
        for blk in range(0, rows, SCAN_BLOCK):
            dst = pl.ds(r0 + blk, SCAN_BLOCK)
            rows_f = state_rows(f_pos + blk)
            rows_r = state_rows(r_pos + blk)
            for j in range(npair):
                cols = slice(j * LANES, (j + 1) * LANES)
                h = hf_ref[j, rows_f, :] + hr_ref[j, rows_r, :]
                y_ref[dst, cols] = (h * g_ref[dst, cols].astype(F32)).astype(BF16)

    def out_tile(i, carry):
        r0 = pl.multiple_of(i * tr, tr)
        write_out(ya_ref, ga_ref, r0, nc + r0, r0, tr)
        return carry

    lax.fori_loop(0, n // tr, out_tile, 0)
    write_out(yac_ref, gac_ref, 0, 0, n, nc)


def _lru(lx, lxc, ga, gac, conv_w, conv_b, wpair, b_a, b_x, lam):
    bsz, n, _ = lx.shape
    nc = lxc.shape[1]
    tr = LRU_ROWS
    lseg, rem = divmod(n + nc, SCAN_SEGMENTS)
    assert rem == 0 and lseg % SCAN_BLOCK == 0 and n % tr == 0 and nc % tr == 0 and tr % SCAN_BLOCK == 0
    npair = LRU_PAIRS_PER_STEP
    wl = npair * LANES
    seq = lambda length: pl.BlockSpec((None, length, wl), lambda b, g: (b, 0, g))
    vec = lambda rows: pl.BlockSpec((rows, wl), lambda b, g: (0, g))
    pad = 2 * SUBLANES
    coeff = pltpu.VMEM((npair, SCAN_SEGMENTS * lseg, LANES), F32)
    state = pltpu.VMEM((npair, SCAN_SEGMENTS * (lseg + SCAN_STATE_PAD), LANES), F32)
    return pl.pallas_call(
        functools.partial(_lru_kernel, n=n, nc=nc, tr=tr),
        grid=(bsz, N_PAIR // npair),
        in_specs=[seq(n), seq(nc), seq(n), seq(nc), vec(conv_w.shape[0]), vec(1),
                  pl.BlockSpec((npair,) + wpair.shape[1:], lambda b, g: (g, 0, 0)), vec(2), vec(2), vec(2)],
        out_specs=[seq(n), seq(nc)],
        out_shape=[jax.ShapeDtypeStruct((bsz, n, W), BF16), jax.ShapeDtypeStruct((bsz, nc, W), BF16)],
        scratch_shapes=[pltpu.VMEM((n + pad, wl), F32), pltpu.VMEM((nc + pad, wl), F32)] + [coeff] * 4 + [state] * 2,
        compiler_params=_params("parallel", "parallel"),
        name="rglru",
    )(lx, lxc, ga, gac, conv_w, conv_b, wpair, b_a, b_x, lam)


def _residual_ln(x, y, gate, g, b):
    z = DEEPNORM_ALPHA * x + gate * y
    mu = jnp.mean(z, axis=-1, keepdims=True)
    zc = z - mu
    var = jnp.mean(zc * zc, axis=-1, keepdims=True)
    return (zc * lax.rsqrt(var + LN_EPS)) * g + b


def _ctx_out_kv_kernel(x_ref, ya_ref, yb_ref, gate_ref, wo_ref, g_ref, b_ref, shift_ref, scale_ref, w_ref,
                       k_ref, v_ref):
    y = _dot(ya_ref[...], wo_ref[0:W, :]) + _dot(yb_ref[...], wo_ref[W:2 * W, :])
    c1 = _residual_ln(x_ref[...], y, gate_ref[...], g_ref[...], b_ref[...])
    h = (c1 * (1.0 + scale_ref[...]) + shift_ref[...]).astype(BF16)
    k_ref[...] = _dot(h, w_ref[:, 0:W]).astype(BF16)
    v_ref[...] = _dot(h, w_ref[:, W:2 * W]).astype(BF16)


def _ctx_out_kv(ctx, ya, yb, gate0, w_out, ln_g, ln_b, shift, scale, w_kv, tm):
    bsz, n, _ = ctx.shape
    tok = lambda width: pl.BlockSpec((None, tm, width), lambda b, i: (b, i, 0))
    return pl.pallas_call(
        _ctx_out_kv_kernel,
        grid=(bsz, n // tm),
        in_specs=[tok(D_MODEL), tok(W), tok(W), _mod_spec(gate0), _bcast_spec(w_out), _bcast_spec(ln_g),
                  _bcast_spec(ln_b), _mod_spec(shift), _mod_spec(scale), _bcast_spec(w_kv)],
        out_specs=[tok(W), tok(W)],
        out_shape=[jax.ShapeDtypeStruct((bsz, n, W), BF16)] * 2,
        compiler_params=_params("parallel", "parallel"),
        name="ctx_out_kv",
    )(ctx, ya, yb, gate0, w_out, ln_g, ln_b, shift, scale, w_kv)


def _rope(t, cos, sin_signed, first_half):
    parts = []
    for j in range(W // LANES):
        tj = t[:, j * LANES:(j + 1) * LANES]
        parts.append(jnp.where(first_half, pltpu.roll(tj, LANES - 16, 1), pltpu.roll(tj, 16, 1)))
    return t * cos + jnp.concatenate(parts, axis=1) * sin_signed


def _ab_out_cd_in_kernel(x_ref, ya_ref, yb_ref, gate_ref, wo_ref, g_ref, b_ref, shift_ref, scale_ref, w_ref,
                         cos_ref, sin_ref, x1_ref, z_ref, cg_ref, q_ref, k_ref, v_ref, ag_ref, *, tm):
    y = _dot(ya_ref[...], wo_ref[0:W, :]) + _dot(yb_ref[...], wo_ref[W:2 * W, :])
    x1 = _residual_ln(x_ref[...], y, gate_ref[...], g_ref[...], b_ref[...])
    x1_ref[...] = x1
    h = (x1 * (1.0 + scale_ref[...]) + shift_ref[...]).astype(BF16)

    def proj(j):
        return _dot(h, w_ref[:, j * W:(j + 1) * W])

    z_ref[...] = proj(2) * proj(0)
    cg_ref[...] = (proj(1) * _silu(proj(3))).astype(BF16)
    cos = cos_ref[...]
    sin = sin_ref[...]
    lane = lax.broadcasted_iota(jnp.int32, (tm, LANES), 1)
    first_half = (lane % 32) < 16
    q_ref[...] = (_rope(proj(4), cos, sin, first_half) * (DIFF_HEAD_DIM ** -0.5 * LOG2_E)).astype(BF16)
    k_ref[...] = _rope(proj(5), cos, sin, first_half).astype(BF16)
    v_ref[...] = proj(6).astype(BF16)
    ag_ref[...] = _silu(proj(7)).astype(BF16)


def _ab_out_cd_in(x, ya, yb, gate0, w_out, ln_g, ln_b, shift, scale, w_in, cos, sin_signed, tm):
    bsz, n, _ = x.shape
    tok = lambda width: pl.BlockSpec((None, tm, width), lambda i, b: (b, i, 0))
    mod = pl.BlockSpec((None, 1, D_MODEL), lambda i, b: (b, 0, 0))
    table = pl.BlockSpec((tm, W), lambda i, b: (i, 0))
    const = lambda a: pl.BlockSpec(a.shape, lambda i, b: (0,) * a.ndim, pipeline_mode=pl.Buffered(1))
    return pl.pallas_call(
        functools.partial(_ab_out_cd_in_kernel, tm=tm),
        grid=(n // tm, bsz),
        in_specs=[tok(D_MODEL), tok(W), tok(W), mod, const(w_out), const(ln_g), const(ln_b), mod, mod, const(w_in),
                  table, table],
        out_specs=[tok(D_MODEL)] + [tok(W)] * 6,
        out_shape=[jax.ShapeDtypeStruct((bsz, n, D_MODEL), F32), jax.ShapeDtypeStruct((bsz, n, W), F32)]
        + [jax.ShapeDtypeStruct((bsz, n, W), BF16)] * 5,
        compiler_params=_params("parallel", "parallel"),
        name="ab_out_cd_in",
    )(x, ya, yb, gate0, w_out, ln_g, ln_b, shift, scale, w_in, cos, sin_signed)


def _attn_out_kernel(q_ref, kc_ref, k_ref, vc_ref, v_ref, ag_ref, x_ref, z_ref, zprev_ref, znext_ref, cg_ref,
                     gate_ref, dl_ref, sg_ref, cw_ref, w_ref, g_ref, b_ref, o_ref,
                     kall_ref, vall_ref, qh_ref, yd_ref, s_ref, p_ref, al_ref, m_ref, acc_ref, *, tq, ck):
    nc = kc_ref.shape[0]
    n = k_ref.shape[0]
    nck = len(ck)
    starts = [sum(ck[:j]) for j in range(nck)]
    i = pl.program_id(1)

    @pl.when(i == 0)
    def _():
        for h in range(DIFF_HEADS):
            cols = slice(h * DIFF_V_DIM, (h + 1) * DIFF_V_DIM)
            kall_ref[h, 0:n, :] = k_ref[:, cols]
            kall_ref[h, n:n + nc, :] = kc_ref[:, cols]
            vall_ref[h, 0:n, 0:DIFF_V_DIM] = v_ref[:, cols]
            vall_ref[h, n:n + nc, 0:DIFF_V_DIM] = vc_ref[:, cols]
            vall_ref[h, :, DIFF_V_DIM:2 * DIFF_V_DIM] = jnp.ones((n + nc, DIFF_V_DIM), BF16)

    dl = dl_ref[...]
    lam = (jnp.exp(jnp.sum(dl[0:1, :] * dl[1:2, :], axis=-1, keepdims=True))
           - jnp.exp(jnp.sum(dl[2:3, :] * dl[3:4, :], axis=-1, keepdims=True)) + LAM_INIT_1)
    lane = lax.broadcasted_iota(jnp.int32, (tq, DIFF_V_DIM), 1)
    for h in range(DIFF_HEADS):
        q = q_ref[:, h * DIFF_V_DIM:(h + 1) * DIFF_V_DIM]
        zero = jnp.zeros_like(q)
        qh_ref[h, 0:tq, :] = jnp.where(lane < DIFF_HEAD_DIM, q, zero)
        qh_ref[h, tq:2 * tq, :] = jnp.where(lane >= DIFF_HEAD_DIM, q, zero)
    nt = (((1,), (1,)), ((), ()))
    slots = s_ref.shape[0]

    def head(h, carry):
        qq = qh_ref[h]

        def scores(j, slot):
            kj = kall_ref[h, starts[j]:starts[j] + ck[j], :]
            s_ref[slot, :, 0:ck[j]] = lax.dot_general(qq, kj, nt, preferred_element_type=F32)

        def softmax(j, slot, first):
            s = s_ref[slot, :, 0:ck[j]]
            sm = s[:, 0:LANES]
            for c in range(1, ck[j] // LANES):
                sm = jnp.maximum(sm, s[:, c * LANES:(c + 1) * LANES])
            m_new = jnp.broadcast_to(jnp.max(sm, axis=-1, keepdims=True), (2 * tq, LANES))
            if not first:
                m_old = m_ref[...]
                m_new = jnp.maximum(m_old, m_new)
                al_ref[slot] = jnp.exp2(m_old - m_new)
            m_ref[...] = m_new
            p_ref[slot, :, 0:ck[j]] = jnp.exp2(s - jnp.concatenate([m_new] * (ck[j] // LANES), axis=1)).astype(BF16)

        def weighted_values(j, slot, first):
            pv = _dot(p_ref[slot, :, 0:ck[j]], vall_ref[h, starts[j]:starts[j] + ck[j], :])
            if first:
                acc_ref[...] = pv
            else:
                alpha = al_ref[slot]
                acc_ref[...] = jnp.concatenate([alpha, alpha], axis=1) * acc_ref[...] + pv

        for t in range(nck + 2):
            if t < nck:
                scores(t, t % slots)
            if 1 <= t <= nck:
                softmax(t - 1, (t - 1) % slots, t == 1)
            if t >= 2:
                weighted_values(t - 2, (t - 2) % slots, t == 2)

        acc = acc_ref[:, 0:DIFF_V_DIM]
        w = 1.0 / acc_ref[:, DIFF_V_DIM:2 * DIFF_V_DIM]
        o = acc[0:tq, :] * w[0:tq, :] - acc[tq:2 * tq, :] * (lam * w[tq:2 * tq, :])
        yd_ref[h] = (o * lax.rsqrt(jnp.mean(o * o, axis=-1, keepdims=True) + LN_EPS)) * sg_ref[...] * (1.0 - LAM_INIT_1)
        return carry

    lax.fori_loop(0, DIFF_HEADS, head, 0, unroll=True)

    yd = jnp.concatenate([yd_ref[h] for h in range(DIFF_HEADS)], axis=1) * ag_ref[...].astype(F32)
    z = z_ref[...]
    row = lax.broadcasted_iota(jnp.int32, (tq, W), 0)
    prev_row = jnp.where(i > 0, zprev_ref[SUBLANES - 1:SUBLANES, :], 0.0)
    next_row = jnp.where(i < pl.num_programs(1) - 1, znext_ref[0:1, :], 0.0)
    z_m1 = jnp.where(row == 0, prev_row, pltpu.roll(z, 1, 0))
    z_p1 = jnp.where(row == tq - 1, next_row, pltpu.roll(z, tq - 1, 0))
    cw = cw_ref[...]
    conv = cw[0:1, :] * z_m1 + cw[1:2, :] * z + cw[2:3, :] * z_p1
    yc = (cg_ref[...].astype(F32) * conv).astype(BF16)
    y = _dot(yc, w_ref[0:W, :]) + _dot(yd.astype(BF16), w_ref[W:2 * W, :])
    o_ref[...] = _residual_ln(x_ref[...], y, gate_ref[...], g_ref[...], b_ref[...])


def _attention_out(q, kc, k, vc, v, ag, x, z, cg, gate, diff_lambda, subln_g, sconv_w, w_out, ln_g, ln_b, tq, ck):
    bsz, n, _ = q.shape
    nc = kc.shape[1]
    assert sum(ck) == n + nc and all(c % (2 * LANES) == 0 for c in ck)
    tok = lambda width: pl.BlockSpec((None, tq, width), lambda b, i: (b, i, 0))
    keys = lambda length: pl.BlockSpec((None, length, W), lambda b, i: (b, 0, 0))
    per = tq // SUBLANES
    last = n // SUBLANES - 1
    zprev = pl.BlockSpec((None, SUBLANES, W), lambda b, i: (b, jnp.maximum(i * per - 1, 0), 0))
    znext = pl.BlockSpec((None, SUBLANES, W), lambda b, i: (b, jnp.minimum((i + 1) * per, last), 0))
    const = lambda a: pl.BlockSpec(a.shape, lambda b, i: (0,) * a.ndim, pipeline_mode=pl.Buffered(1))
    consts = [diff_lambda, subln_g, sconv_w, w_out, ln_g, ln_b]
    slots = 2
    return pl.pallas_call(
        functools.partial(_attn_out_kernel, tq=tq, ck=ck),
        grid=(bsz, n // tq),
        in_specs=[tok(W), keys(nc), keys(n), keys(nc), keys(n), tok(W), tok(D_MODEL), tok(W), zprev, znext, tok(W),
                  _mod_spec(gate)] + [const(a) for a in consts],
        out_specs=tok(D_MODEL),
        out_shape=jax.ShapeDtypeStruct((bsz, n, D_MODEL), F32),
        scratch_shapes=[pltpu.VMEM((DIFF_HEADS, n + nc, DIFF_V_DIM), BF16),
                        pltpu.VMEM((DIFF_HEADS, n + nc, 2 * DIFF_V_DIM), BF16),
                        pltpu.VMEM((DIFF_HEADS, 2 * tq, DIFF_V_DIM), BF16),
                        pltpu.VMEM((DIFF_HEADS, tq, DIFF_V_DIM), F32),
                        pltpu.VMEM((slots, 2 * tq, max(ck)), F32), pltpu.VMEM((slots, 2 * tq, max(ck)), BF16),
                        pltpu.VMEM((slots, 2 * tq, LANES), F32), pltpu.VMEM((2 * tq, LANES), F32),
                        pltpu.VMEM((2 * tq, 2 * DIFF_V_DIM), F32)],
        compiler_params=_params("parallel", "arbitrary"),
        name="attention_out",
    )(q, kc, k, vc, v, ag, x, z, z, z, cg, gate, *consts)


def _pair_gate_weights(w_a, w_x):
    def pair_blockdiag(w):
        wp = w.reshape(N_PAIR, 2, LRU_HEAD_DIM, LRU_HEAD_DIM)
        zero = jnp.zeros_like(wp[:, 0])
        top = jnp.concatenate([wp[:, 0], zero], axis=-1)
        bot = jnp.concatenate([zero, wp[:, 1]], axis=-1)
        return jnp.concatenate([top, bot], axis=-2)
    return jnp.concatenate([pair_blockdiag(w_a[0]), pair_blockdiag(w_x[0]),
                            pair_blockdiag(w_a[1]), pair_blockdiag(w_x[1])], axis=-1).astype(BF16)


def _rope_tables(n):
    rows = n // GRID_W
    row = jnp.repeat(jnp.arange(rows, dtype=F32), GRID_W)
    col = jnp.tile(jnp.arange(GRID_W, dtype=F32), rows)
    n_freq = DIFF_HEAD_DIM // 4
    inv = ROPE_BASE ** (-jnp.arange(n_freq, dtype=F32) / n_freq)
    ang_r = row[:, None] * inv
    ang_c = col[:, None] * inv
    ang = jnp.concatenate([ang_r, ang_r, ang_c, ang_c], axis=-1)
    sign = jnp.tile(jnp.concatenate([-jnp.ones((n_freq,), F32), jnp.ones((n_freq,), F32)]), 2)
    reps = W // DIFF_HEAD_DIM
    return jnp.tile(jnp.cos(ang), (1, reps)), jnp.tile(jnp.sin(ang) * sign, (1, reps))


def kernel(x, c, ctx, c_ctx, w_mod, b_mod, ln_g, ln_b, ab_w_in, ab_w_out, lru_conv_w, lru_conv_b, lru_w_a, lru_b_a,
           lru_w_x, lru_b_x, lru_lambda, sgu_ln_g, sgu_ln_b, sgu_w, sgu_b, cd_w_in, cd_w_out, sconv_w, diff_lambda,
           diff_subln_g):
    bsz, n, _ = x.shape
    nc = ctx.shape[1]
    tiles = _tiling(n, nc)
    row = lambda a: a.reshape(1, -1)

    n_cond = -(-(bsz + 1) // SUBLANES) * SUBLANES
    cond = jnp.zeros((n_cond, D_MODEL), F32).at[:bsz].set(c).at[bsz].set(c_ctx)
    mod = _modulation(cond, w_mod, b_mod)

    def mods(l):
        lat = [mod[l, :bsz, j * D_MODEL:(j + 1) * D_MODEL][:, None, :] for j in range(3)]
        con = [mod[l, bsz:bsz + 1, j * D_MODEL:(j + 1) * D_MODEL][:, None, :] for j in range(3)]
        return lat, con

    (shift, scale, gate), (shift_c, scale_c, gate_c) = mods(0)
    w_in = ab_w_in[0].astype(BF16)
    w_out = ab_w_out[0].astype(BF16)
    sgu_args = (row(sgu_ln_g[0]), row(sgu_ln_b[0]), sgu_w[0].astype(BF16),
                jnp.broadcast_to(sgu_b[0][:, :, None], (SGU_GROUPS, SGU_CHUNK, LANES)))
    lx, ga, yb = _ab_in(x, shift, scale, w_in, *sgu_args, tm=tiles.proj_rows)
    lxc, gac, ybc = _ab_in(ctx, shift_c, scale_c, w_in, *sgu_args, tm=tiles.ctx_rows)
    wpair = _pair_gate_weights(lru_w_a[0], lru_w_x[0])
    ya, yac = _lru(lx, lxc, ga, gac, lru_conv_w[0], row(lru_conv_b[0]), wpair, lru_b_a[0], lru_b_x[0], lru_lambda[0])

    (shift, scale, gate1), (shift_c, scale_c, _) = mods(1)
    w_in = cd_w_in[0].astype(BF16)
    cos, sin_signed = _rope_tables(n)
    x1, z, cg, q, k, v, ag = _ab_out_cd_in(x, ya, yb, gate, w_out, row(ln_g[0]), row(ln_b[0]), shift, scale, w_in,
                                           cos, sin_signed, tm=tiles.proj_rows)
    kc, vc = _ctx_out_kv(ctx, yac, ybc, gate_c, w_out, row(ln_g[0]), row(ln_b[0]), shift_c, scale_c,
                         w_in[:, 5 * W:7 * W], tm=tiles.ctx_rows)
    gate = gate1
    w_out = cd_w_out[0].astype(BF16)
    return _attention_out(q, kc, k, vc, v, ag, x1, z, cg, gate, diff_lambda[0], row(diff_subln_g[0]), sconv_w[0], w_out,
                          row(ln_g[1]), row(ln_b[1]), tq=tiles.query_rows, ck=tiles.key_chunks)
```

```python
import functools
import math
from typing import NamedTuple

import jax
import jax.numpy as jnp
from jax import lax
from jax.experimental import pallas as pl
from jax.experimental.pallas import tpu as pltpu

D_MODEL = 1024
DEPTH = 2
GRID_W = 64
W = D_MODEL // 2
LRU_HEAD_DIM = 64
LRU_C = 8.0
LRU_PAIR = 2 * LRU_HEAD_DIM
N_PAIR = W // LRU_PAIR
SGU_CHUNK = 128
SGU_GROUPS = 4
DIFF_V_DIM = 128
DIFF_HEADS = W // DIFF_V_DIM
DIFF_HEAD_DIM = DIFF_V_DIM // 2
ROPE_BASE = 10000.0
DEEPNORM_ALPHA = (2 * DEPTH) ** 0.25
LN_EPS = 1e-5
LAM_INIT_1 = 0.8 - 0.6 * math.exp(-0.3 * 1)
LOG2_E = math.log2(math.e)

SUBLANES = 8
LANES = 128
VMEM_LIMIT = 56 * 1024 * 1024

F32 = jnp.float32
BF16 = jnp.bfloat16

MXU_DIM = 256
PROJ_ROWS = 2 * MXU_DIM
KEY_CHUNK = 4 * MXU_DIM
LRU_ROWS = MXU_DIM
LRU_PAIRS_PER_STEP = 2


class _Tiling(NamedTuple):
    proj_rows: int
    ctx_rows: int
    query_rows: int
    key_chunks: tuple


def _tiling(n, nc):
    assert n % PROJ_ROWS == 0 and n % KEY_CHUNK == 0 and nc % MXU_DIM == 0 and nc <= KEY_CHUNK
    return _Tiling(PROJ_ROWS, nc, PROJ_ROWS, (KEY_CHUNK,) * (n // KEY_CHUNK) + (nc,))


def _silu(x):
    return x * jax.nn.sigmoid(x)


def _dot(a, b):
    return jnp.dot(a, b, preferred_element_type=F32)


def _params(*sem):
    return pltpu.CompilerParams(dimension_semantics=sem, vmem_limit_bytes=VMEM_LIMIT)


def _bcast_spec(arr):
    zeros = (0,) * arr.ndim
    return pl.BlockSpec(arr.shape, lambda *_: zeros)


def _mod_spec(arr):
    if arr.shape[0] == 1:
        return pl.BlockSpec((None, 1, D_MODEL), lambda b, i: (0, 0, 0))
    return pl.BlockSpec((None, 1, D_MODEL), lambda b, i: (b, 0, 0))


def _mod_kernel(cond_ref, w_ref, b_ref, o_ref):
    s = _silu(cond_ref[...]).astype(BF16)
    o_ref[...] = _dot(s, w_ref[...].astype(BF16)) + b_ref[...]


def _modulation(cond, w_mod, b_mod):
    n = cond.shape[0]
    return pl.pallas_call(
        _mod_kernel,
        grid=(DEPTH, 3),
        in_specs=[
            pl.BlockSpec((n, D_MODEL), lambda l, j: (0, 0)),
            pl.BlockSpec((None, D_MODEL, D_MODEL), lambda l, j: (l, 0, j)),
            pl.BlockSpec((None, 1, D_MODEL), lambda l, j: (l, 0, j)),
        ],
        out_specs=pl.BlockSpec((None, n, D_MODEL), lambda l, j: (l, 0, j)),
        out_shape=jax.ShapeDtypeStruct((DEPTH, n, 3 * D_MODEL), F32),
        compiler_params=_params("arbitrary", "arbitrary"),
        name="modulation",
    )(cond, w_mod, b_mod.reshape(DEPTH, 1, 3 * D_MODEL))


def _ab_in_kernel(x_ref, shift_ref, scale_ref, w_ref, lng_ref, lnb_ref, ws_ref, bs_ref,
                  lx_ref, ga_ref, yb_ref, *, tm):
    h = (x_ref[...] * (1.0 + scale_ref[...]) + shift_ref[...]).astype(BF16)

    def proj(j):
        return _dot(h, w_ref[:, j * W:(j + 1) * W])

    lx_ref[...] = proj(0)
    ga_ref[...] = _silu(proj(1)).astype(BF16)
    u = jax.nn.gelu(proj(2))
    v = jax.nn.gelu(proj(3))
    mu = jnp.mean(v, axis=-1, keepdims=True)
    vc = v - mu
    var = jnp.mean(vc * vc, axis=-1, keepdims=True)
    vn = ((vc * lax.rsqrt(var + LN_EPS)) * lng_ref[...] + lnb_ref[...]).astype(BF16)
    ug = u * _silu(proj(4))
    for c in range(tm // SGU_CHUNK):
        rows = slice(c * SGU_CHUNK, (c + 1) * SGU_CHUNK)
        for g in range(SGU_GROUPS):
            cols = slice(g * LANES, (g + 1) * LANES)
            s = _dot(ws_ref[g], vn[rows, cols]) + bs_ref[g]
            yb_ref[rows, cols] = (ug[rows, cols] * s).astype(BF16)


def _ab_in(x, shift, scale, w_in, sgu_ln_g, sgu_ln_b, sgu_w, sgu_b, tm):
    bsz, n, _ = x.shape
    tok = lambda width: pl.BlockSpec((None, tm, width), lambda b, i: (b, i, 0))
    consts = [w_in, sgu_ln_g, sgu_ln_b, sgu_w, sgu_b]
    return pl.pallas_call(
        functools.partial(_ab_in_kernel, tm=tm),
        grid=(bsz, n // tm),
        in_specs=[tok(D_MODEL), _mod_spec(shift), _mod_spec(scale)] + [_bcast_spec(a) for a in consts],
        out_specs=[tok(W), tok(W), tok(W)],
        out_shape=[jax.ShapeDtypeStruct((bsz, n, W), F32),
                   jax.ShapeDtypeStruct((bsz, n, W), BF16),
                   jax.ShapeDtypeStruct((bsz, n, W), BF16)],
        compiler_params=_params("parallel", "parallel"),
        name="ab_in_sgu",
    )(x, shift, scale, *consts)


SCAN_SEGMENTS = SUBLANES
SCAN_BLOCK = 32
SCAN_STATE_PAD = SUBLANES


def _lru_kernel(lx_ref, lxc_ref, ga_ref, gac_ref, cw_ref, cb_ref, wp_ref, ba_ref, bx_ref, lam_ref,
                ya_ref, yac_ref, xp_ref, xpc_ref, af_ref, bf_ref, ar_ref, br_ref, hf_ref, hr_ref, *, n, nc, tr):
    nt = n + nc
    lseg = nt // SCAN_SEGMENTS
    h_pitch = lseg + SCAN_STATE_PAD
    pad = SUBLANES
    wl = lx_ref.shape[-1]
    npair = wl // LANES
    zeros = jnp.zeros((pad, wl), F32)
    for ref, src, length in ((xp_ref, lx_ref, n), (xpc_ref, lxc_ref, nc)):
        ref[0:pad, :] = zeros
        ref[pad + length:2 * pad + length, :] = zeros
        ref[pad:pad + length, :] = src[...]

    lam = lam_ref[...]
    neg = -lam
    softplus = jnp.maximum(neg, 0.0) + jnp.log1p(jnp.exp(-jnp.abs(neg)))
    rate = LRU_C * softplus
    rate_log2 = -LOG2_E * rate
    cw = cw_ref[...]
    cb = cb_ref[...]

    def block_rows(pos):
        if isinstance(pos, int):
            seg, step = divmod(pos, lseg)
        else:
            seg = lax.div(pos, lseg)
            step = pos - seg * lseg
        return pl.ds(step * SCAN_SEGMENTS + seg, SCAN_BLOCK, stride=SCAN_SEGMENTS)

    def coeffs(src_ref, r0, f_off, r_off):
        xc = cb + sum(cw[k:k + 1, :] * src_ref[r0 + pad - 2 + k:r0 + pad - 2 + k + tr, :] for k in range(4))
        xcb = xc.astype(BF16)
        dst = [[block_rows(off + r0 + blk) for blk in range(0, tr, SCAN_BLOCK)] for off in (f_off, r_off)]
        for j in range(npair):
            cols = slice(j * LANES, (j + 1) * LANES)
            gts = _dot(xcb[:, cols], wp_ref[j])
            xj = xc[:, cols]
            for d, (a_ref, b_ref) in enumerate(((af_ref, bf_ref), (ar_ref, br_ref))):
                gate_r = jax.nn.sigmoid(gts[:, (2 * d) * LANES:(2 * d + 1) * LANES] + ba_ref[d:d + 1, cols])
                gate_i = jax.nn.sigmoid(gts[:, (2 * d + 1) * LANES:(2 * d + 2) * LANES] + bx_ref[d:d + 1, cols])
                a = jnp.exp2(gate_r * rate_log2[d:d + 1, cols])
                one_minus_a2 = jnp.tanh(gate_r * rate[d:d + 1, cols]) * (1.0 + a * a)
                root = jnp.where(one_minus_a2 > 0.0, one_minus_a2 * lax.rsqrt(one_minus_a2), 0.0)
                b = root * (gate_i * xj)
                for rows, blk in zip(dst[d], range(0, tr, SCAN_BLOCK)):
                    a_ref[j, rows, :] = a[blk:blk + SCAN_BLOCK, :]
                    b_ref[j, rows, :] = b[blk:blk + SCAN_BLOCK, :]

    for i in range(n // tr):
        coeffs(xp_ref, i * tr, nc, 0)
    for i in range(nc // tr):
        coeffs(xpc_ref, i * tr, 0, n)

    chains = [(a_ref, b_ref, h_ref, j, rev)
              for a_ref, b_ref, h_ref, rev in ((af_ref, bf_ref, hf_ref, False), (ar_ref, br_ref, hr_ref, True))
              for j in range(npair)]

    def step_rows(t, rev):
        return pl.ds(pl.multiple_of((lseg - 1 - t if rev else t) * SCAN_SEGMENTS, SCAN_SEGMENTS), SCAN_SEGMENTS)

    def totals_step(t, carry):
        out = []
        for (a_ref, b_ref, _, j, rev), (decay, h) in zip(chains, carry):
            rows = step_rows(t, rev)
            a = a_ref[j, rows, :]
            out.append((a * decay, a * h + b_ref[j, rows, :]))
        return tuple(out)

    one = jnp.ones((SCAN_SEGMENTS, LANES), F32)
    zero = jnp.zeros((SCAN_SEGMENTS, LANES), F32)
    totals = lax.fori_loop(0, lseg, totals_step, tuple((one, zero) for _ in chains), unroll=8)

    def entering_state(decay, h_end, rev):
        order = range(SCAN_SEGMENTS - 1, -1, -1) if rev else range(SCAN_SEGMENTS)
        rows = [None] * SCAN_SEGMENTS
        state = jnp.zeros((1, LANES), F32)
        for s in order:
            rows[s] = state
            state = decay[s:s + 1, :] * state + h_end[s:s + 1, :]
        return jnp.concatenate(rows, axis=0)

    starts = tuple(entering_state(decay, h_end, chain[-1]) for chain, (decay, h_end) in zip(chains, totals))

    def states_step(t, carry):
        out = []
        for (a_ref, b_ref, h_ref, j, rev), h in zip(chains, carry):
            rows = step_rows(t, rev)
            h = a_ref[j, rows, :] * h + b_ref[j, rows, :]
            h_ref[j, pl.ds(lseg - 1 - t if rev else t, SCAN_SEGMENTS, stride=h_pitch), :] = h
            out.append(h)
        return tuple(out)

    lax.fori_loop(0, lseg, states_step, starts, unroll=8)

    def state_rows(pos):
        if isinstance(pos, int):
            seg, step = divmod(pos, lseg)
            return pl.ds(seg * h_pitch + step, SCAN_BLOCK)
        seg = lax.div(pos, lseg)
        return pl.ds(pl.multiple_of(seg * h_pitch + (pos - seg * lseg), SUBLANES), SCAN_BLOCK)

    def write_out(y_ref, g_ref, r0, f_pos, r_pos, rows):
        for blk in range(0, rows, SCAN_BLOCK):
            dst = pl.ds(r0 + blk, SCAN_BLOCK)
            rows_f = state_rows(f_pos + blk)
            rows_r = state_rows(r_pos + blk)
            for j in range(npair):
                cols = slice(j * LANES, (j + 1) * LANES)
                h = hf_ref[j, rows_f, :] + hr_ref[j, rows_r, :]
                y_ref[dst, cols] = (h * g_ref[dst, cols].astype(F32)).astype(BF16)

    write_out(ya_ref, ga_ref, 0, nc, 0, n)
    write_out(yac_ref, gac_ref, 0, 0, n, nc)


def _lru(lx, lxc, ga, gac, conv_w, conv_b, wpair, b_a, b_x, lam):
    bsz, n, _ = lx.shape
    nc = lxc.shape[1]
    tr = LRU_ROWS
    lseg, rem = divmod(n + nc, SCAN_SEGMENTS)
    assert rem == 0 and lseg % SCAN_BLOCK == 0 and n % tr == 0 and nc % tr == 0 and tr % SCAN_BLOCK == 0
    npair = LRU_PAIRS_PER_STEP
    wl = npair * LANES
    seq = lambda length: pl.BlockSpec((None, length, wl), lambda b, g: (b, 0, g))
    vec = lambda rows: pl.BlockSpec((rows, wl), lambda b, g: (0, g))
    pad = 2 * SUBLANES
    coeff = pltpu.VMEM((npair, SCAN_SEGMENTS * lseg, LANES), F32)
    state = pltpu.VMEM((npair, SCAN_SEGMENTS * (lseg + SCAN_STATE_PAD), LANES), F32)
    return pl.pallas_call(
        functools.partial(_lru_kernel, n=n, nc=nc, tr=tr),
        grid=(bsz, N_PAIR // npair),
        in_specs=[seq(n), seq(nc), seq(n), seq(nc), vec(conv_w.shape[0]), vec(1),
                  pl.BlockSpec((npair,) + wpair.shape[1:], lambda b, g: (g, 0, 0)), vec(2), vec(2), vec(2)],
        out_specs=[seq(n), seq(nc)],
        out_shape=[jax.ShapeDtypeStruct((bsz, n, W), BF16), jax.ShapeDtypeStruct((bsz, nc, W), BF16)],
        scratch_shapes=[pltpu.VMEM((n + pad, wl), F32), pltpu.VMEM((nc + pad, wl), F32)] + [coeff] * 4 + [state] * 2,
        compiler_params=_params("parallel", "parallel"),
        name="rglru",
    )(lx, lxc, ga, gac, conv_w, conv_b, wpair, b_a, b_x, lam)


def _residual_ln(x, y, gate, g, b):
    z = DEEPNORM_ALPHA * x + gate * y
    mu = jnp.mean(z, axis=-1, keepdims=True)
    zc = z - mu
    var = jnp.mean(zc * zc, axis=-1, keepdims=True)
    return (zc * lax.rsqrt(var + LN_EPS)) * g + b


def _ctx_out_kv_kernel(x_ref, ya_ref, yb_ref, gate_ref, wo_ref, g_ref, b_ref, shift_ref, scale_ref, w_ref,
                       k_ref, v_ref):
    y = _dot(ya_ref[...], wo_ref[0:W, :]) + _dot(yb_ref[...], wo_ref[W:2 * W, :])
    c1 = _residual_ln(x_ref[...], y, gate_ref[...], g_ref[...], b_ref[...])
    h = (c1 * (1.0 + scale_ref[...]) + shift_ref[...]).astype(BF16)
    k_ref[...] = _dot(h, w_ref[:, 0:W]).astype(BF16)
    v_ref[...] = _dot(h, w_ref[:, W:2 * W]).astype(BF16)


def _ctx_out_kv(ctx, ya, yb, gate0, w_out, ln_g, ln_b, shift, scale, w_kv, tm):
    bsz, n, _ = ctx.shape
    tok = lambda width: pl.BlockSpec((None, tm, width), lambda b, i: (b, i, 0))
    return pl.pallas_call(
        _ctx_out_kv_kernel,
        grid=(bsz, n // tm),
        in_specs=[tok(D_MODEL), tok(W), tok(W), _mod_spec(gate0), _bcast_spec(w_out), _bcast_spec(ln_g),
                  _bcast_spec(ln_b), _mod_spec(shift), _mod_spec(scale), _bcast_spec(w_kv)],
        out_specs=[tok(W), tok(W)],
        out_shape=[jax.ShapeDtypeStruct((bsz, n, W), BF16)] * 2,
        compiler_params=_params("parallel", "parallel"),
        name="ctx_out_kv",
    )(ctx, ya, yb, gate0, w_out, ln_g, ln_b, shift, scale, w_kv)


def _rope(t, cos, sin_signed, first_half):
    parts = []
    for j in range(W // LANES):
        tj = t[:, j * LANES:(j + 1) * LANES]
        parts.append(jnp.where(first_half, pltpu.roll(tj, LANES - 16, 1), pltpu.roll(tj, 16, 1)))
    return t * cos + jnp.concatenate(parts, axis=1) * sin_signed


def _ab_out_cd_in_kernel(x_ref, ya_ref, yb_ref, gate_ref, wo_ref, g_ref, b_ref, shift_ref, scale_ref, w_ref,
                         cos_ref, sin_ref, x1_ref, z_ref, cg_ref, q_ref, k_ref, v_ref, ag_ref, *, tm):
    y = _dot(ya_ref[...], wo_ref[0:W, :]) + _dot(yb_ref[...], wo_ref[W:2 * W, :])
    x1 = _residual_ln(x_ref[...], y, gate_ref[...], g_ref[...], b_ref[...])
    x1_ref[...] = x1
    h = (x1 * (1.0 + scale_ref[...]) + shift_ref[...]).astype(BF16)

    def proj(j):
        return _dot(h, w_ref[:, j * W:(j + 1) * W])

    z_ref[...] = proj(2) * proj(0)
    cg_ref[...] = (proj(1) * _silu(proj(3))).astype(BF16)
    cos = cos_ref[...]
    sin = sin_ref[...]
    lane = lax.broadcasted_iota(jnp.int32, (tm, LANES), 1)
    first_half = (lane % 32) < 16
    q_ref[...] = (_rope(proj(4), cos, sin, first_half) * (DIFF_HEAD_DIM ** -0.5 * LOG2_E)).astype(BF16)
    k_ref[...] = _rope(proj(5), cos, sin, first_half).astype(BF16)
    v_ref[...] = proj(6).astype(BF16)
    ag_ref[...] = _silu(proj(7)).astype(BF16)


def _ab_out_cd_in(x, ya, yb, gate0, w_out, ln_g, ln_b, shift, scale, w_in, cos, sin_signed, tm):
    bsz, n, _ = x.shape
    tok = lambda width: pl.BlockSpec((None, tm, width), lambda i, b: (b, i, 0))
    mod = pl.BlockSpec((None, 1, D_MODEL), lambda i, b: (b, 0, 0))
    table = pl.BlockSpec((tm, W), lambda i, b: (i, 0))
    const = lambda a: pl.BlockSpec(a.shape, lambda i, b: (0,) * a.ndim, pipeline_mode=pl.Buffered(1))
    return pl.pallas_call(
        functools.partial(_ab_out_cd_in_kernel, tm=tm),
        grid=(n // tm, bsz),
        in_specs=[tok(D_MODEL), tok(W), tok(W), mod, const(w_out), const(ln_g), const(ln_b), mod, mod, const(w_in),
                  table, table],
        out_specs=[tok(D_MODEL)] + [tok(W)] * 6,
        out_shape=[jax.ShapeDtypeStruct((bsz, n, D_MODEL), F32), jax.ShapeDtypeStruct((bsz, n, W), F32)]
        + [jax.ShapeDtypeStruct((bsz, n, W), BF16)] * 5,
        compiler_params=_params("parallel", "parallel"),
        name="ab_out_cd_in",
    )(x, ya, yb, gate0, w_out, ln_g, ln_b, shift, scale, w_in, cos, sin_signed)


def _attn_out_kernel(q_ref, kc_ref, k_ref, vc_ref, v_ref, ag_ref, x_ref, z_ref, zprev_ref, znext_ref, cg_ref,
                     gate_ref, dl_ref, sg_ref, cw_ref, w_ref, g_ref, b_ref, o_ref,
                     kall_ref, vall_ref, qh_ref, yd_ref, s_ref, p_ref, al_ref, m_ref, acc_ref, *, tq, ck):
    nc = kc_ref.shape[0]
    n = k_ref.shape[0]
    nck = len(ck)
    starts = [sum(ck[:j]) for j in range(nck)]
    i = pl.program_id(1)

    @pl.when(i == 0)
    def _():
        for h in range(DIFF_HEADS):
            cols = slice(h * DIFF_V_DIM, (h + 1) * DIFF_V_DIM)
            kall_ref[h, 0:n, :] = k_ref[:, cols]
            kall_ref[h, n:n + nc, :] = kc_ref[:, cols]
            vall_ref[h, 0:n, 0:DIFF_V_DIM] = v_ref[:, cols]
            vall_ref[h, n:n + nc, 0:DIFF_V_DIM] = vc_ref[:, cols]
            vall_ref[h, :, DIFF_V_DIM:2 * DIFF_V_DIM] = jnp.ones((n + nc, DIFF_V_DIM), BF16)

    dl = dl_ref[...]
    lam = (jnp.exp(jnp.sum(dl[0:1, :] * dl[1:2, :], axis=-1, keepdims=True))
           - jnp.exp(jnp.sum(dl[2:3, :] * dl[3:4, :], axis=-1, keepdims=True)) + LAM_INIT_1)
    lane = lax.broadcasted_iota(jnp.int32, (tq, DIFF_V_DIM), 1)
    for h in range(DIFF_HEADS):
        q = q_ref[:, h * DIFF_V_DIM:(h + 1) * DIFF_V_DIM]
        zero = jnp.zeros_like(q)
        qh_ref[h, 0:tq, :] = jnp.where(lane < DIFF_HEAD_DIM, q, zero)
        qh_ref[h, tq:2 * tq, :] = jnp.where(lane >= DIFF_HEAD_DIM, q, zero)
    nt = (((1,), (1,)), ((), ()))
    slots = s_ref.shape[0]

    def head(h, carry):
        qq = qh_ref[h]

        def scores(j, slot):
            kj = kall_ref[h, starts[j]:starts[j] + ck[j], :]
            s_ref[slot, :, 0:ck[j]] = lax.dot_general(qq, kj, nt, preferred_element_type=F32)

        def softmax(j, slot, first):
            s = s_ref[slot, :, 0:ck[j]]
            sm = s[:, 0:LANES]
            for c in range(1, ck[j] // LANES):
                sm = jnp.maximum(sm, s[:, c * LANES:(c + 1) * LANES])
            m_new = jnp.broadcast_to(jnp.max(sm, axis=-1, keepdims=True), (2 * tq, LANES))
            if not first:
                m_old = m_ref[...]
                m_new = jnp.maximum(m_old, m_new)
                al_ref[slot] = jnp.exp2(m_old - m_new)
            m_ref[...] = m_new
            p_ref[slot, :, 0:ck[j]] = jnp.exp2(s - jnp.concatenate([m_new] * (ck[j] // LANES), axis=1)).astype(BF16)

        def weighted_values(j, slot, first):
            pv = _dot(p_ref[slot, :, 0:ck[j]], vall_ref[h, starts[j]:starts[j] + ck[j], :])
            if first:
                acc_ref[...] = pv
            else:
                alpha = al_ref[slot]
                acc_ref[...] = jnp.concatenate([alpha, alpha], axis=1) * acc_ref[...] + pv

        for t in range(nck + 2):
            if t < nck:
                scores(t, t % slots)
            if 1 <= t <= nck:
                softmax(t - 1, (t - 1) % slots, t == 1)
            if t >= 2:
                weighted_values(t - 2, (t - 2) % slots, t == 2)

        acc = acc_ref[:, 0:DIFF_V_DIM]
        w = 1.0 / acc_ref[:, DIFF_V_DIM:2 * DIFF_V_DIM]
        o = acc[0:tq, :] * w[0:tq, :] - acc[tq:2 * tq, :] * (lam * w[tq:2 * tq, :])
        yd_ref[h] = (o * lax.rsqrt(jnp.mean(o * o, axis=-1, keepdims=True) + LN_EPS)) * sg_ref[...] * (1.0 - LAM_INIT_1)
        return carry

    lax.fori_loop(0, DIFF_HEADS, head, 0, unroll=True)

    yd = jnp.concatenate([yd_ref[h] for h in range(DIFF_HEADS)], axis=1) * ag_ref[...].astype(F32)
    z = z_ref[...]
    row = lax.broadcasted_iota(jnp.int32, (tq, W), 0)
    prev_row = jnp.where(i > 0, zprev_ref[SUBLANES - 1:SUBLANES, :], 0.0)
    next_row = jnp.where(i < pl.num_programs(1) - 1, znext_ref[0:1, :], 0.0)
    z_m1 = jnp.where(row == 0, prev_row, pltpu.roll(z, 1, 0))
    z_p1 = jnp.where(row == tq - 1, next_row, pltpu.roll(z, tq - 1, 0))
    cw = cw_ref[...]
    conv = cw[0:1, :] * z_m1 + cw[1:2, :] * z + cw[2:3, :] * z_p1
    yc = (cg_ref[...].astype(F32) * conv).astype(BF16)
    y = _dot(yc, w_ref[0:W, :]) + _dot(yd.astype(BF16), w_ref[W:2 * W, :])
    o_ref[...] = _residual_ln(x_ref[...], y, gate_ref[...], g_ref[...], b_ref[...])


def _attention_out(q, kc, k, vc, v, ag, x, z, cg, gate, diff_lambda, subln_g, sconv_w, w_out, ln_g, ln_b, tq, ck):
    bsz, n, _ = q.shape
    nc = kc.shape[1]
    assert sum(ck) == n + nc and all(c % (2 * LANES) == 0 for c in ck)
    tok = lambda width: pl.BlockSpec((None, tq, width), lambda b, i: (b, i, 0))
    keys = lambda length: pl.BlockSpec((None, length, W), lambda b, i: (b, 0, 0))
    per = tq // SUBLANES
    last = n // SUBLANES - 1
    zprev = pl.BlockSpec((None, SUBLANES, W), lambda b, i: (b, jnp.maximum(i * per - 1, 0), 0))
    znext = pl.BlockSpec((None, SUBLANES, W), lambda b, i: (b, jnp.minimum((i + 1) * per, last), 0))
    const = lambda a: pl.BlockSpec(a.shape, lambda b, i: (0,) * a.ndim, pipeline_mode=pl.Buffered(1))
    consts = [diff_lambda, subln_g, sconv_w, w_out, ln_g, ln_b]
    slots = 2
    return pl.pallas_call(
        functools.partial(_attn_out_kernel, tq=tq, ck=ck),
        grid=(bsz, n // tq),
        in_specs=[tok(W), keys(nc), keys(n), keys(nc), keys(n), tok(W), tok(D_MODEL), tok(W), zprev, znext, tok(W),
                  _mod_spec(gate)] + [const(a) for a in consts],
        out_specs=tok(D_MODEL),
        out_shape=jax.ShapeDtypeStruct((bsz, n, D_MODEL), F32),
        scratch_shapes=[pltpu.VMEM((DIFF_HEADS, n + nc, DIFF_V_DIM), BF16),
                        pltpu.VMEM((DIFF_HEADS, n + nc, 2 * DIFF_V_DIM), BF16),
                        pltpu.VMEM((DIFF_HEADS, 2 * tq, DIFF_V_DIM), BF16),
                        pltpu.VMEM((DIFF_HEADS, tq, DIFF_V_DIM), F32),
                        pltpu.VMEM((slots, 2 * tq, max(ck)), F32), pltpu.VMEM((slots, 2 * tq, max(ck)), BF16),
                        pltpu.VMEM((slots, 2 * tq, LANES), F32), pltpu.VMEM((2 * tq, LANES), F32),
                        pltpu.VMEM((2 * tq, 2 * DIFF_V_DIM), F32)],
        compiler_params=_params("parallel", "arbitrary"),
        name="attention_out",
    )(q, kc, k, vc, v, ag, x, z, z, z, cg, gate, *consts)


def _pair_gate_weights(w_a, w_x):
    def pair_blockdiag(w):
        wp = w.reshape(N_PAIR, 2, LRU_HEAD_DIM, LRU_HEAD_DIM)
        zero = jnp.zeros_like(wp[:, 0])
        top = jnp.concatenate([wp[:, 0], zero], axis=-1)
        bot = jnp.concatenate([zero, wp[:, 1]], axis=-1)
        return jnp.concatenate([top, bot], axis=-2)
    return jnp.concatenate([pair_blockdiag(w_a[0]), pair_blockdiag(w_x[0]),
                            pair_blockdiag(w_a[1]), pair_blockdiag(w_x[1])], axis=-1).astype(BF16)


def _rope_tables(n):
    rows = n // GRID_W
    row = jnp.repeat(jnp.arange(rows, dtype=F32), GRID_W)
    col = jnp.tile(jnp.arange(GRID_W, dtype=F32), rows)
    n_freq = DIFF_HEAD_DIM // 4
    inv = ROPE_BASE ** (-jnp.arange(n_freq, dtype=F32) / n_freq)
    ang_r = row[:, None] * inv
    ang_c = col[:, None] * inv
    ang = jnp.concatenate([ang_r, ang_r, ang_c, ang_c], axis=-1)
    sign = jnp.tile(jnp.concatenate([-jnp.ones((n_freq,), F32), jnp.ones((n_freq,), F32)]), 2)
    reps = W // DIFF_HEAD_DIM
    return jnp.tile(jnp.cos(ang), (1, reps)), jnp.tile(jnp.sin(ang) * sign, (1, reps))


def kernel(x, c, ctx, c_ctx, w_mod, b_mod, ln_g, ln_b, ab_w_in, ab_w_out, lru_conv_w, lru_conv_b, lru_w_a, lru_b_a,
           lru_w_x, lru_b_x, lru_lambda, sgu_ln_g, sgu_ln_b, sgu_w, sgu_b, cd_w_in, cd_w_out, sconv_w, diff_lambda,
           diff_subln_g):
    bsz, n, _ = x.shape
    nc = ctx.shape[1]
    tiles = _tiling(n, nc)
    row = lambda a: a.reshape(1, -1)

    n_cond = -(-(bsz + 1) // SUBLANES) * SUBLANES
    cond = jnp.zeros((n_cond, D_MODEL), F32).at[:bsz].set(c).at[bsz].set(c_ctx)
    mod = _modulation(cond, w_mod, b_mod)

    def mods(l):
        lat = [mod[l, :bsz, j * D_MODEL:(j + 1) * D_MODEL][:, None, :] for j in range(3)]
        con = [mod[l, bsz:bsz + 1, j * D_MODEL:(j + 1) * D_MODEL][:, None, :] for j in range(3)]
        return lat, con

    (shift, scale, gate), (shift_c, scale_c, gate_c) = mods(0)
    w_in = ab_w_in[0].astype(BF16)
    w_out = ab_w_out[0].astype(BF16)
    sgu_args = (row(sgu_ln_g[0]), row(sgu_ln_b[0]), sgu_w[0].astype(BF16),
                jnp.broadcast_to(sgu_b[0][:, :, None], (SGU_GROUPS, SGU_CHUNK, LANES)))
    lx, ga, yb = _ab_in(x, shift, scale, w_in, *sgu_args, tm=tiles.proj_rows)
    lxc, gac, ybc = _ab_in(ctx, shift_c, scale_c, w_in, *sgu_args, tm=tiles.ctx_rows)
    wpair = _pair_gate_weights(lru_w_a[0], lru_w_x[0])
    ya, yac = _lru(lx, lxc, ga, gac, lru_conv_w[0], row(lru_conv_b[0]), wpair, lru_b_a[0], lru_b_x[0], lru_lambda[0])

    (shift, scale, gate1), (shift_c, scale_c, _) = mods(1)
    w_in = cd_w_in[0].astype(BF16)
    cos, sin_signed = _rope_tables(n)
    x1, z, cg, q, k, v, ag = _ab_out_cd_in(x, ya, yb, gate, w_out, row(ln_g[0]), row(ln_b[0]), shift, scale, w_in,
                                           cos, sin_signed, tm=tiles.proj_rows)
    kc, vc = _ctx_out_kv(ctx, yac, ybc, gate_c, w_out, row(ln_g[0]), row(ln_b[0]), shift_c, scale_c,
                         w_in[:, 5 * W:7 * W], tm=tiles.ctx_rows)
    gate = gate1
    w_out = cd_w_out[0].astype(BF16)
    return _attention_out(q, kc, k, vc, v, ag, x1, z, cg, gate, diff_lambda[0], row(diff_subln_g[0]), sconv_w[0], w_out,
                          row(ln_g[1]), row(ln_b[1]), tq=tiles.query_rows, ck=tiles.key_chunks)
```

```python
import functools
import math
from typing import NamedTuple

import jax
import jax.numpy as jnp
from jax import lax
from jax.experimental import pallas as pl
from jax.experimental.pallas import tpu as pltpu

D_MODEL = 1024
DEPTH = 2
GRID_W = 64
W = D_MODEL // 2
LRU_HEAD_DIM = 64
LRU_C = 8.0
LRU_PAIR = 2 * LRU_HEAD_DIM
N_PAIR = W // LRU_PAIR
SGU_CHUNK = 128
SGU_GROUPS = 4
DIFF_V_DIM = 128
DIFF_HEADS = W // DIFF_V_DIM
DIFF_HEAD_DIM = DIFF_V_DIM // 2
ROPE_BASE = 10000.0
DEEPNORM_ALPHA = (2 * DEPTH) ** 0.25
LN_EPS = 1e-5
LAM_INIT_1 = 0.8 - 0.6 * math.exp(-0.3 * 1)
LOG2_E = math.log2(math.e)

SUBLANES = 8
LANES = 128
VMEM_LIMIT = 56 * 1024 * 1024

F32 = jnp.float32
BF16 = jnp.bfloat16

MXU_DIM = 256
PROJ_ROWS = 2 * MXU_DIM
KEY_CHUNK = 4 * MXU_DIM
LRU_ROWS = MXU_DIM
LRU_PAIRS_PER_STEP = 2


class _Tiling(NamedTuple):
    proj_rows: int
    ctx_rows: int
    query_rows: int
    key_chunks: tuple


def _tiling(n, nc):
    assert n % PROJ_ROWS == 0 and n % KEY_CHUNK == 0 and nc % MXU_DIM == 0 and nc <= KEY_CHUNK
    return _Tiling(PROJ_ROWS, nc, PROJ_ROWS, (KEY_CHUNK,) * (n // KEY_CHUNK) + (nc,))


def _silu(x):
    return x * jax.nn.sigmoid(x)


def _dot(a, b):
    return jnp.dot(a, b, preferred_element_type=F32)


def _params(*sem):
    return pltpu.CompilerParams(dimension_semantics=sem, vmem_limit_bytes=VMEM_LIMIT)


def _bcast_spec(arr):
    zeros = (0,) * arr.ndim
    return pl.BlockSpec(arr.shape, lambda *_: zeros)


def _mod_spec(arr):
    if arr.shape[0] == 1:
        return pl.BlockSpec((None, 1, D_MODEL), lambda b, i: (0, 0, 0))
    return pl.BlockSpec((None, 1, D_MODEL), lambda b, i: (b, 0, 0))


def _mod_kernel(cond_ref, w_ref, b_ref, o_ref):
    s = _silu(cond_ref[...]).astype(BF16)
    o_ref[...] = _dot(s, w_ref[...].astype(BF16)) + b_ref[...]


def _modulation(cond, w_mod, b_mod):
    n = cond.shape[0]
    return pl.pallas_call(
        _mod_kernel,
        grid=(DEPTH, 3),
        in_specs=[
            pl.BlockSpec((n, D_MODEL), lambda l, j: (0, 0)),
            pl.BlockSpec((None, D_MODEL, D_MODEL), lambda l, j: (l, 0, j)),
            pl.BlockSpec((None, 1, D_MODEL), lambda l, j: (l, 0, j)),
        ],
        out_specs=pl.BlockSpec((None, n, D_MODEL), lambda l, j: (l, 0, j)),
        out_shape=jax.ShapeDtypeStruct((DEPTH, n, 3 * D_MODEL), F32),
        compiler_params=_params("arbitrary", "arbitrary"),
        name="modulation",
    )(cond, w_mod, b_mod.reshape(DEPTH, 1, 3 * D_MODEL))


def _ab_in_kernel(x_ref, shift_ref, scale_ref, w_ref, lng_ref, lnb_ref, ws_ref, bs_ref,
                  lx_ref, ga_ref, yb_ref, *, tm):
    h = (x_ref[...] * (1.0 + scale_ref[...]) + shift_ref[...]).astype(BF16)

    def proj(j):
        return _dot(h, w_ref[:, j * W:(j + 1) * W])

    lx_ref[...] = proj(0)
    ga_ref[...] = _silu(proj(1)).astype(BF16)
    u = jax.nn.gelu(proj(2))
    v = jax.nn.gelu(proj(3))
    mu = jnp.mean(v, axis=-1, keepdims=True)
    vc = v - mu
    var = jnp.mean(vc * vc, axis=-1, keepdims=True)
    vn = ((vc * lax.rsqrt(var + LN_EPS)) * lng_ref[...] + lnb_ref[...]).astype(BF16)
    ug = u * _silu(proj(4))
    for c in range(tm // SGU_CHUNK):
        rows = slice(c * SGU_CHUNK, (c + 1) * SGU_CHUNK)
        for g in range(SGU_GROUPS):
            cols = slice(g * LANES, (g + 1) * LANES)
            s = _dot(ws_ref[g], vn[rows, cols]) + bs_ref[g]
            yb_ref[rows, cols] = (ug[rows, cols] * s).astype(BF16)


def _ab_in(x, shift, scale, w_in, sgu_ln_g, sgu_ln_b, sgu_w, sgu_b, tm):
    bsz, n, _ = x.shape
    tok = lambda width: pl.BlockSpec((None, tm, width), lambda b, i: (b, i, 0))
    consts = [w_in, sgu_ln_g, sgu_ln_b, sgu_w, sgu_b]
    return pl.pallas_call(
        functools.partial(_ab_in_kernel, tm=tm),
        grid=(bsz, n // tm),
        in_specs=[tok(D_MODEL), _mod_spec(shift), _mod_spec(scale)] + [_bcast_spec(a) for a in consts],
        out_specs=[tok(W), tok(W), tok(W)],
        out_shape=[jax.ShapeDtypeStruct((bsz, n, W), F32),
                   jax.ShapeDtypeStruct((bsz, n, W), BF16),
                   jax.ShapeDtypeStruct((bsz, n, W), BF16)],
        compiler_params=_params("parallel", "parallel"),
        name="ab_in_sgu",
    )(x, shift, scale, *consts)


SCAN_SEGMENTS = SUBLANES
SCAN_BLOCK = 32
SCAN_STATE_PAD = SUBLANES


def _lru_kernel(lx_ref, lxc_ref, ga_ref, gac_ref, cw_ref, cb_ref, wp_ref, ba_ref, bx_ref, lam_ref,
                ya_ref, yac_ref, xp_ref, xpc_ref, af_ref, bf_ref, ar_ref, br_ref, hf_ref, hr_ref, *, n, nc, tr):
    nt = n + nc
    lseg = nt // SCAN_SEGMENTS
    h_pitch = lseg + SCAN_STATE_PAD
    pad = SUBLANES
    wl = lx_ref.shape[-1]
    npair = wl // LANES
    zeros = jnp.zeros((pad, wl), F32)
    for ref, src, length in ((xp_ref, lx_ref, n), (xpc_ref, lxc_ref, nc)):
        ref[0:pad, :] = zeros
        ref[pad + length:2 * pad + length, :] = zeros
        ref[pad:pad + length, :] = src[...]

    lam = lam_ref[...]
    neg = -lam
    softplus = jnp.maximum(neg, 0.0) + jnp.log1p(jnp.exp(-jnp.abs(neg)))
    rate = LRU_C * softplus
    rate_log2 = -LOG2_E * rate
    cw = cw_ref[...]
    cb = cb_ref[...]

    def block_rows(pos):
        if isinstance(pos, int):
            seg, step = divmod(pos, lseg)
        else:
            seg = lax.div(pos, lseg)
            step = pos - seg * lseg
        return pl.ds(step * SCAN_SEGMENTS + seg, SCAN_BLOCK, stride=SCAN_SEGMENTS)

    def coeffs(src_ref, r0, f_off, r_off):
        xc = cb + sum(cw[k:k + 1, :] * src_ref[r0 + pad - 2 + k:r0 + pad - 2 + k + tr, :] for k in range(4))
        xcb = xc.astype(BF16)
        dst = [[block_rows(off + r0 + blk) for blk in range(0, tr, SCAN_BLOCK)] for off in (f_off, r_off)]
        for j in range(npair):
            cols = slice(j * LANES, (j + 1) * LANES)
            gts = _dot(xcb[:, cols], wp_ref[j])
            xj = xc[:, cols]
            for d, (a_ref, b_ref) in enumerate(((af_ref, bf_ref), (ar_ref, br_ref))):
                gate_r = jax.nn.sigmoid(gts[:, (2 * d) * LANES:(2 * d + 1) * LANES] + ba_ref[d:d + 1, cols])
                gate_i = jax.nn.sigmoid(gts[:, (2 * d + 1) * LANES:(2 * d + 2) * LANES] + bx_ref[d:d + 1, cols])
                a = jnp.exp2(gate_r * rate_log2[d:d + 1, cols])
                one_minus_a2 = jnp.tanh(gate_r * rate[d:d + 1, cols]) * (1.0 + a * a)
                root = jnp.where(one_minus_a2 > 0.0, one_minus_a2 * lax.rsqrt(one_minus_a2), 0.0)
                b = root * (gate_i * xj)
                for rows, blk in zip(dst[d], range(0, tr, SCAN_BLOCK)):
                    a_ref[j, rows, :] = a[blk:blk + SCAN_BLOCK, :]
                    b_ref[j, rows, :] = b[blk:blk + SCAN_BLOCK, :]

    for i in range(n // tr):
        coeffs(xp_ref, i * tr, nc, 0)
    for i in range(nc // tr):
        coeffs(xpc_ref, i * tr, 0, n)

    chains = [(a_ref, b_ref, h_ref, j, rev)
              for a_ref, b_ref, h_ref, rev in ((af_ref, bf_ref, hf_ref, False), (ar_ref, br_ref, hr_ref, True))
              for j in range(npair)]

    def step_rows(t, rev):
        return pl.ds(pl.multiple_of((lseg - 1 - t if rev else t) * SCAN_SEGMENTS, SCAN_SEGMENTS), SCAN_SEGMENTS)

    def totals_step(t, carry):
        out = []
        for (a_ref, b_ref, _, j, rev), (decay, h) in zip(chains, carry):
            rows = step_rows(t, rev)
            a = a_ref[j, rows, :]
            out.append((a * decay, a * h + b_ref[j, rows, :]))
        return tuple(out)

    one = jnp.ones((SCAN_SEGMENTS, LANES), F32)
    zero = jnp.zeros((SCAN_SEGMENTS, LANES), F32)
    totals = lax.fori_loop(0, lseg, totals_step, tuple((one, zero) for _ in chains), unroll=True)

    def entering_state(decay, h_end, rev):
        order = range(SCAN_SEGMENTS - 1, -1, -1) if rev else range(SCAN_SEGMENTS)
        rows = [None] * SCAN_SEGMENTS
        state = jnp.zeros((1, LANES), F32)
        for s in order:
            rows[s] = state
            state = decay[s:s + 1, :] * state + h_end[s:s + 1, :]
        return jnp.concatenate(rows, axis=0)

    starts = tuple(entering_state(decay, h_end, chain[-1]) for chain, (decay, h_end) in zip(chains, totals))

    def states_step(t, carry):
        out = []
        for (a_ref, b_ref, h_ref, j, rev), h in zip(chains, carry):
            rows = step_rows(t, rev)
            h = a_ref[j, rows, :] * h + b_ref[j, rows, :]
            h_ref[j, pl.ds(lseg - 1 - t if rev else t, SCAN_SEGMENTS, stride=h_pitch), :] = h
            out.append(h)
        return tuple(out)

    lax.fori_loop(0, lseg, states_step, starts, unroll=True)

    def state_rows(pos):
        if isinstance(pos, int):
            seg, step = divmod(pos, lseg)
            return pl.ds(seg * h_pitch + step, SCAN_BLOCK)
        seg = lax.div(pos, lseg)
        return pl.ds(pl.multiple_of(seg * h_pitch + (pos - seg * lseg), SUBLANES), SCAN_BLOCK)

    def write_out(y_ref, g_ref, r0, f_pos, r_pos, rows):
        for blk in range(0, rows, SCAN_BLOCK):
            dst = pl.ds(r0 + blk, SCAN_BLOCK)
            rows_f = state_rows(f_pos + blk)
            rows_r = state_rows(r_pos + blk)
            for j in range(npair):
                cols = slice(j * LANES, (j + 1) * LANES)
                h = hf_ref[j, rows_f, :] + hr_ref[j, rows_r, :]
                y_ref[dst, cols] = (h * g_ref[dst, cols].astype(F32)).astype(BF16)

    write_out(ya_ref, ga_ref, 0, nc, 0, n)
    write_out(yac_ref, gac_ref, 0, 0, n, nc)


def _lru(lx, lxc, ga, gac, conv_w, conv_b, wpair, b_a, b_x, lam):
    bsz, n, _ = lx.shape
    nc = lxc.shape[1]
    tr = LRU_ROWS
    lseg, rem = divmod(n + nc, SCAN_SEGMENTS)
    assert rem == 0 and lseg % SCAN_BLOCK == 0 and n % tr == 0 and nc % tr == 0 and tr % SCAN_BLOCK == 0
    npair = LRU_PAIRS_PER_STEP
    wl = npair * LANES
    seq = lambda length: pl.BlockSpec((None, length, wl), lambda b, g: (b, 0, g))
    vec = lambda rows: pl.BlockSpec((rows, wl), lambda b, g: (0, g))
    pad = 2 * SUBLANES
    coeff = pltpu.VMEM((npair, SCAN_SEGMENTS * lseg, LANES), F32)
    state = pltpu.VMEM((npair, SCAN_SEGMENTS * (lseg + SCAN_STATE_PAD), LANES), F32)
    return pl.pallas_call(
        functools.partial(_lru_kernel, n=n, nc=nc, tr=tr),
        grid=(bsz, N_PAIR // npair),
        in_specs=[seq(n), seq(nc), seq(n), seq(nc), vec(conv_w.shape[0]), vec(1),
                  pl.BlockSpec((npair,) + wpair.shape[1:], lambda b, g: (g, 0, 0)), vec(2), vec(2), vec(2)],
        out_specs=[seq(n), seq(nc)],
        out_shape=[jax.ShapeDtypeStruct((bsz, n, W), BF16), jax.ShapeDtypeStruct((bsz, nc, W), BF16)],
        scratch_shapes=[pltpu.VMEM((n + pad, wl), F32), pltpu.VMEM((nc + pad, wl), F32)] + [coeff] * 4 + [state] * 2,
        compiler_params=_params("parallel", "parallel"),
        name="rglru",
    )(lx, lxc, ga, gac, conv_w, conv_b, wpair, b_a, b_x, lam)


def _residual_ln(x, y, gate, g, b):
    z = DEEPNORM_ALPHA * x + gate * y
    mu = jnp.mean(z, axis=-1, keepdims=True)
    zc = z - mu
    var = jnp.mean(zc * zc, axis=-1, keepdims=True)
    return (zc * lax.rsqrt(var + LN_EPS)) * g + b


def _ctx_out_kv_kernel(x_ref, ya_ref, yb_ref, gate_ref, wo_ref, g_ref, b_ref, shift_ref, scale_ref, w_ref,
                       k_ref, v_ref):
    y = _dot(ya_ref[...], wo_ref[0:W, :]) + _dot(yb_ref[...], wo_ref[W:2 * W, :])
    c1 = _residual_ln(x_ref[...], y, gate_ref[...], g_ref[...], b_ref[...])
    h = (c1 * (1.0 + scale_ref[...]) + shift_ref[...]).astype(BF16)
    k_ref[...] = _dot(h, w_ref[:, 0:W]).astype(BF16)
    v_ref[...] = _dot(h, w_ref[:, W:2 * W]).astype(BF16)


def _ctx_out_kv(ctx, ya, yb, gate0, w_out, ln_g, ln_b, shift, scale, w_kv, tm):
    bsz, n, _ = ctx.shape
    tok = lambda width: pl.BlockSpec((None, tm, width), lambda b, i: (b, i, 0))
    return pl.pallas_call(
        _ctx_out_kv_kernel,
        grid=(bsz, n // tm),
        in_specs=[tok(D_MODEL), tok(W), tok(W), _mod_spec(gate0), _bcast_spec(w_out), _bcast_spec(ln_g),
                  _bcast_spec(ln_b), _mod_spec(shift), _mod_spec(scale), _bcast_spec(w_kv)],
        out_specs=[tok(W), tok(W)],
        out_shape=[jax.ShapeDtypeStruct((bsz, n, W), BF16)] * 2,
        compiler_params=_params("parallel", "parallel"),
        name="ctx_out_kv",
    )(ctx, ya, yb, gate0, w_out, ln_g, ln_b, shift, scale, w_kv)


def _rope(t, cos, sin_signed, first_half):
    parts = []
    for j in range(W // LANES):
        tj = t[:, j * LANES:(j + 1) * LANES]
        parts.append(jnp.where(first_half, pltpu.roll(tj, LANES - 16, 1), pltpu.roll(tj, 16, 1)))
    return t * cos + jnp.concatenate(parts, axis=1) * sin_signed


def _ab_out_cd_in_kernel(x_ref, ya_ref, yb_ref, gate_ref, wo_ref, g_ref, b_ref, shift_ref, scale_ref, w_ref,
                         cos_ref, sin_ref, x1_ref, z_ref, cg_ref, q_ref, k_ref, v_ref, ag_ref, *, tm):
    y = _dot(ya_ref[...], wo_ref[0:W, :]) + _dot(yb_ref[...], wo_ref[W:2 * W, :])
    x1 = _residual_ln(x_ref[...], y, gate_ref[...], g_ref[...], b_ref[...])
    x1_ref[...] = x1
    h = (x1 * (1.0 + scale_ref[...]) + shift_ref[...]).astype(BF16)

    def proj(j):
        return _dot(h, w_ref[:, j * W:(j + 1) * W])

    z_ref[...] = proj(2) * proj(0)
    cg_ref[...] = (proj(1) * _silu(proj(3))).astype(BF16)
    cos = cos_ref[...]
    sin = sin_ref[...]
    lane = lax.broadcasted_iota(jnp.int32, (tm, LANES), 1)
    first_half = (lane % 32) < 16
    q_ref[...] = (_rope(proj(4), cos, sin, first_half) * (DIFF_HEAD_DIM ** -0.5 * LOG2_E)).astype(BF16)
    k_ref[...] = _rope(proj(5), cos, sin, first_half).astype(BF16)
    v_ref[...] = proj(6).astype(BF16)
    ag_ref[...] = _silu(proj(7)).astype(BF16)


def _ab_out_cd_in(x, ya, yb, gate0, w_out, ln_g, ln_b, shift, scale, w_in, cos, sin_signed, tm):
    bsz, n, _ = x.shape
    tok = lambda width: pl.BlockSpec((None, tm, width), lambda i, b: (b, i, 0))
    mod = pl.BlockSpec((None, 1, D_MODEL), lambda i, b: (b, 0, 0))
    table = pl.BlockSpec((tm, W), lambda i, b: (i, 0))
    const = lambda a: pl.BlockSpec(a.shape, lambda i, b: (0,) * a.ndim, pipeline_mode=pl.Buffered(1))
    return pl.pallas_call(
        functools.partial(_ab_out_cd_in_kernel, tm=tm),
        grid=(n // tm, bsz),
        in_specs=[tok(D_MODEL), tok(W), tok(W), mod, const(w_out), const(ln_g), const(ln_b), mod, mod, const(w_in),
                  table, table],
        out_specs=[tok(D_MODEL)] + [tok(W)] * 6,
        out_shape=[jax.ShapeDtypeStruct((bsz, n, D_MODEL), F32), jax.ShapeDtypeStruct((bsz, n, W), F32)]
        + [jax.ShapeDtypeStruct((bsz, n, W), BF16)] * 5,
        compiler_params=_params("parallel", "parallel"),
        name="ab_out_cd_in",
    )(x, ya, yb, gate0, w_out, ln_g, ln_b, shift, scale, w_in, cos, sin_signed)


def _attn_out_kernel(q_ref, kc_ref, k_ref, vc_ref, v_ref, ag_ref, x_ref, z_ref, zprev_ref, znext_ref, cg_ref,
                     gate_ref, dl_ref, sg_ref, cw_ref, w_ref, g_ref, b_ref, o_ref,
                     kall_ref, vall_ref, qh_ref, yd_ref, s_ref, p_ref, al_ref, m_ref, acc_ref, *, tq, ck):
    nc = kc_ref.shape[0]
    n = k_ref.shape[0]
    nck = len(ck)
    starts = [sum(ck[:j]) for j in range(nck)]
    i = pl.program_id(1)

    @pl.when(i == 0)
    def _():
        for h in range(DIFF_HEADS):
            cols = slice(h * DIFF_V_DIM, (h + 1) * DIFF_V_DIM)
            kall_ref[h, 0:n, :] = k_ref[:, cols]
            kall_ref[h, n:n + nc, :] = kc_ref[:, cols]
            vall_ref[h, 0:n, 0:DIFF_V_DIM] = v_ref[:, cols]
            vall_ref[h, n:n + nc, 0:DIFF_V_DIM] = vc_ref[:, cols]
            vall_ref[h, :, DIFF_V_DIM:2 * DIFF_V_DIM] = jnp.ones((n + nc, DIFF_V_DIM), BF16)

    dl = dl_ref[...]
    lam = (jnp.exp(jnp.sum(dl[0:1, :] * dl[1:2, :], axis=-1, keepdims=True))
           - jnp.exp(jnp.sum(dl[2:3, :] * dl[3:4, :], axis=-1, keepdims=True)) + LAM_INIT_1)
    lane = lax.broadcasted_iota(jnp.int32, (tq, DIFF_V_DIM), 1)
    for h in range(DIFF_HEADS):
        q = q_ref[:, h * DIFF_V_DIM:(h + 1) * DIFF_V_DIM]
        zero = jnp.zeros_like(q)
        qh_ref[h, 0:tq, :] = jnp.where(lane < DIFF_HEAD_DIM, q, zero)
        qh_ref[h, tq:2 * tq, :] = jnp.where(lane >= DIFF_HEAD_DIM, q, zero)
    nt = (((1,), (1,)), ((), ()))
    slots = s_ref.shape[0]

    def head(h, carry):
        qq = qh_ref[h]

        def scores(j, slot):
            kj = kall_ref[h, starts[j]:starts[j] + ck[j], :]
            s_ref[slot, :, 0:ck[j]] = lax.dot_general(qq, kj, nt, preferred_element_type=F32)

        def softmax(j, slot, first):
            s = s_ref[slot, :, 0:ck[j]]
            sm = s[:, 0:LANES]
            for c in range(1, ck[j] // LANES):
                sm = jnp.maximum(sm, s[:, c * LANES:(c + 1) * LANES])
            m_new = jnp.broadcast_to(jnp.max(sm, axis=-1, keepdims=True), (2 * tq, LANES))
            if not first:
                m_old = m_ref[...]
                m_new = jnp.maximum(m_old, m_new)
                al_ref[slot] = jnp.exp2(m_old - m_new)
            m_ref[...] = m_new
            p_ref[slot, :, 0:ck[j]] = jnp.exp2(s - jnp.concatenate([m_new] * (ck[j] // LANES), axis=1)).astype(BF16)

        def weighted_values(j, slot, first):
            pv = _dot(p_ref[slot, :, 0:ck[j]], vall_ref[h, starts[j]:starts[j] + ck[j], :])
            if first:
                acc_ref[...] = pv
            else:
                alpha = al_ref[slot]
                acc_ref[...] = jnp.concatenate([alpha, alpha], axis=1) * acc_ref[...] + pv

        for t in range(nck + 2):
            if t < nck:
                scores(t, t % slots)
            if 1 <= t <= nck:
                softmax(t - 1, (t - 1) % slots, t == 1)
            if t >= 2:
                weighted_values(t - 2, (t - 2) % slots, t == 2)

        acc = acc_ref[:, 0:DIFF_V_DIM]
        w = 1.0 / acc_ref[:, DIFF_V_DIM:2 * DIFF_V_DIM]
        o = acc[0:tq, :] * w[0:tq, :] - acc[tq:2 * tq, :] * (lam * w[tq:2 * tq, :])
        yd_ref[h] = (o * lax.rsqrt(jnp.mean(o * o, axis=-1, keepdims=True) + LN_EPS)) * sg_ref[...] * (1.0 - LAM_INIT_1)
        return carry

    lax.fori_loop(0, DIFF_HEADS, head, 0, unroll=True)

    yd = jnp.concatenate([yd_ref[h] for h in range(DIFF_HEADS)], axis=1) * ag_ref[...].astype(F32)
    z = z_ref[...]
    row = lax.broadcasted_iota(jnp.int32, (tq, W), 0)
    prev_row = jnp.where(i > 0, zprev_ref[SUBLANES - 1:SUBLANES, :], 0.0)
    next_row = jnp.where(i < pl.num_programs(1) - 1, znext_ref[0:1, :], 0.0)
    z_m1 = jnp.where(row == 0, prev_row, pltpu.roll(z, 1, 0))
    z_p1 = jnp.where(row == tq - 1, next_row, pltpu.roll(z, tq - 1, 0))
    cw = cw_ref[...]
    conv = cw[0:1, :] * z_m1 + cw[1:2, :] * z + cw[2:3, :] * z_p1
    yc = (cg_ref[...].astype(F32) * conv).astype(BF16)
    y = _dot(yc, w_ref[0:W, :]) + _dot(yd.astype(BF16), w_ref[W:2 * W, :])
    o_ref[...] = _residual_ln(x_ref[...], y, gate_ref[...], g_ref[...], b_ref[...])


def _attention_out(q, kc, k, vc, v, ag, x, z, cg, gate, diff_lambda, subln_g, sconv_w, w_out, ln_g, ln_b, tq, ck):
    bsz, n, _ = q.shape
    nc = kc.shape[1]
    assert sum(ck) == n + nc and all(c % (2 * LANES) == 0 for c in ck)
    tok = lambda width: pl.BlockSpec((None, tq, width), lambda b, i: (b, i, 0))
    keys = lambda length: pl.BlockSpec((None, length, W), lambda b, i: (b, 0, 0))
    per = tq // SUBLANES
    last = n // SUBLANES - 1
    zprev = pl.BlockSpec((None, SUBLANES, W), lambda b, i: (b, jnp.maximum(i * per - 1, 0), 0))
    znext = pl.BlockSpec((None, SUBLANES, W), lambda b, i: (b, jnp.minimum((i + 1) * per, last), 0))
    const = lambda a: pl.BlockSpec(a.shape, lambda b, i: (0,) * a.ndim, pipeline_mode=pl.Buffered(1))
    consts = [diff_lambda, subln_g, sconv_w, w_out, ln_g, ln_b]
    slots = 2
    return pl.pallas_call(
        functools.partial(_attn_out_kernel, tq=tq, ck=ck),
        grid=(bsz, n // tq),
        in_specs=[tok(W), keys(nc), keys(n), keys(nc), keys(n), tok(W), tok(D_MODEL), tok(W), zprev, znext, tok(W),
                  _mod_spec(gate)] + [const(a) for a in consts],
        out_specs=tok(D_MODEL),
        out_shape=jax.ShapeDtypeStruct((bsz, n, D_MODEL), F32),
        scratch_shapes=[pltpu.VMEM((DIFF_HEADS, n + nc, DIFF_V_DIM), BF16),
                        pltpu.VMEM((DIFF_HEADS, n + nc, 2 * DIFF_V_DIM), BF16),
                        pltpu.VMEM((DIFF_HEADS, 2 * tq, DIFF_V_DIM), BF16),
                        pltpu.VMEM((DIFF_HEADS, tq, DIFF_V_DIM), F32),
                        pltpu.VMEM((slots, 2 * tq, max(ck)), F32), pltpu.VMEM((slots, 2 * tq, max(ck)), BF16),
                        pltpu.VMEM((slots, 2 * tq, LANES), F32), pltpu.VMEM((2 * tq, LANES), F32),
                        pltpu.VMEM((2 * tq, 2 * DIFF_V_DIM), F32)],
        compiler_params=_params("parallel", "arbitrary"),
        name="attention_out",
    )(q, kc, k, vc, v, ag, x, z, z, z, cg, gate, *consts)


def _pair_gate_weights(w_a, w_x):
    def pair_blockdiag(w):
        wp = w.reshape(N_PAIR, 2, LRU_HEAD_DIM, LRU_HEAD_DIM)
        zero = jnp.zeros_like(wp[:, 0])
        top = jnp.concatenate([wp[:, 0], zero], axis=-1)
        bot = jnp.concatenate([zero, wp[:, 1]], axis=-1)
        return jnp.concatenate([top, bot], axis=-2)
    return jnp.concatenate([pair_blockdiag(w_a[0]), pair_blockdiag(w_x[0]),
                            pair_blockdiag(w_a[1]), pair_blockdiag(w_x[1])], axis=-1).astype(BF16)


def _rope_tables(n):
    rows = n // GRID_W
    row = jnp.repeat(jnp.arange(rows, dtype=F32), GRID_W)
    col = jnp.tile(jnp.arange(GRID_W, dtype=F32), rows)
    n_freq = DIFF_HEAD_DIM // 4
    inv = ROPE_BASE ** (-jnp.arange(n_freq, dtype=F32) / n_freq)
    ang_r = row[:, None] * inv
    ang_c = col[:, None] * inv
    ang = jnp.concatenate([ang_r, ang_r, ang_c, ang_c], axis=-1)
    sign = jnp.tile(jnp.concatenate([-jnp.ones((n_freq,), F32), jnp.ones((n_freq,), F32)]), 2)
    reps = W // DIFF_HEAD_DIM
    return jnp.tile(jnp.cos(ang), (1, reps)), jnp.tile(jnp.sin(ang) * sign, (1, reps))


def kernel(x, c, ctx, c_ctx, w_mod, b_mod, ln_g, ln_b, ab_w_in, ab_w_out, lru_conv_w, lru_conv_b, lru_w_a, lru_b_a,
           lru_w_x, lru_b_x, lru_lambda, sgu_ln_g, sgu_ln_b, sgu_w, sgu_b, cd_w_in, cd_w_out, sconv_w, diff_lambda,
           diff_subln_g):
    bsz, n, _ = x.shape
    nc = ctx.shape[1]
    tiles = _tiling(n, nc)
    row = lambda a: a.reshape(1, -1)

    n_cond = -(-(bsz + 1) // SUBLANES) * SUBLANES
    cond = jnp.zeros((n_cond, D_MODEL), F32).at[:bsz].set(c).at[bsz].set(c_ctx)
    mod = _modulation(cond, w_mod, b_mod)

    def mods(l):
        lat = [mod[l, :bsz, j * D_MODEL:(j + 1) * D_MODEL][:, None, :] for j in range(3)]
        con = [mod[l, bsz:bsz + 1, j * D_MODEL:(j + 1) * D_MODEL][:, None, :] for j in range(3)]
        return lat, con

    (shift, scale, gate), (shift_c, scale_c, gate_c) = mods(0)
    w_in = ab_w_in[0].astype(BF16)
    w_out = ab_w_out[0].astype(BF16)
    sgu_args = (row(sgu_ln_g[0]), row(sgu_ln_b[0]), sgu_w[0].astype(BF16),
                jnp.broadcast_to(sgu_b[0][:, :, None], (SGU_GROUPS, SGU_CHUNK, LANES)))
    lx, ga, yb = _ab_in(x, shift, scale, w_in, *sgu_args, tm=tiles.proj_rows)
    lxc, gac, ybc = _ab_in(ctx, shift_c, scale_c, w_in, *sgu_args, tm=tiles.ctx_rows)
    wpair = _pair_gate_weights(lru_w_a[0], lru_w_x[0])
    ya, yac = _lru(lx, lxc, ga, gac, lru_conv_w[0], row(lru_conv_b[0]), wpair, lru_b_a[0], lru_b_x[0], lru_lambda[0])

    (shift, scale, gate1), (shift_c, scale_c, _) = mods(1)
    w_in = cd_w_in[0].astype(BF16)
    cos, sin_signed = _rope_tables(n)
    x1, z, cg, q, k, v, ag = _ab_out_cd_in(x, ya, yb, gate, w_out, row(ln_g[0]), row(ln_b[0]), shift, scale, w_in,
                                           cos, sin_signed, tm=tiles.proj_rows)
    kc, vc = _ctx_out_kv(ctx, yac, ybc, gate_c, w_out, row(ln_g[0]), row(ln_b[0]), shift_c, scale_c,
                         w_in[:, 5 * W:7 * W], tm=tiles.ctx_rows)
    gate = gate1
    w_out = cd_w_out[0].astype(BF16)
    return _attention_out(q, kc, k, vc, v, ag, x1, z, cg, gate, diff_lambda[0], row(diff_subln_g[0]), sconv_w[0], w_out,
                          row(ln_g[1]), row(ln_b[1]), tq=tiles.query_rows, ck=tiles.key_chunks)
```

```python
import functools
import math
from typing import NamedTuple

import jax
import jax.numpy as jnp
from jax import lax
from jax.experimental import pallas as pl
from jax.experimental.pallas import tpu as pltpu

D_MODEL = 1024
DEPTH = 2
GRID_W = 64
W = D_MODEL // 2
LRU_HEAD_DIM = 64
LRU_C = 8.0
LRU_PAIR = 2 * LRU_HEAD_DIM
N_PAIR = W // LRU_PAIR
SGU_CHUNK = 128
SGU_GROUPS = 4
DIFF_V_DIM = 128
DIFF_HEADS = W // DIFF_V_DIM
DIFF_HEAD_DIM = DIFF_V_DIM // 2
ROPE_BASE = 10000.0
DEEPNORM_ALPHA = (2 * DEPTH) ** 0.25
LN_EPS = 1e-5
LAM_INIT_1 = 0.8 - 0.6 * math.exp(-0.3 * 1)
LOG2_E = math.log2(math.e)

SUBLANES = 8
LANES = 128
VMEM_LIMIT = 56 * 1024 * 1024

F32 = jnp.float32
BF16 = jnp.bfloat16

MXU_DIM = 256
PROJ_ROWS = 2 * MXU_DIM
KEY_CHUNK = 4 * MXU_DIM
LRU_ROWS = MXU_DIM
LRU_PAIRS_PER_STEP = 2


class _Tiling(NamedTuple):
    proj_rows: int
    ctx_rows: int
    query_rows: int
    key_chunks: tuple


def _tiling(n, nc):
    assert n % PROJ_ROWS == 0 and n % KEY_CHUNK == 0 and nc % MXU_DIM == 0 and nc <= KEY_CHUNK
    return _Tiling(PROJ_ROWS, nc, PROJ_ROWS, (KEY_CHUNK,) * (n // KEY_CHUNK) + (nc,))


def _silu(x):
    return x * jax.nn.sigmoid(x)


def _dot(a, b):
    return jnp.dot(a, b, preferred_element_type=F32)


def _params(*sem):
    return pltpu.CompilerParams(dimension_semantics=sem, vmem_limit_bytes=VMEM_LIMIT)


def _bcast_spec(arr):
    zeros = (0,) * arr.ndim
    return pl.BlockSpec(arr.shape, lambda *_: zeros)


def _mod_spec(arr):
    if arr.shape[0] == 1:
        return pl.BlockSpec((None, 1, D_MODEL), lambda b, i: (0, 0, 0))
    return pl.BlockSpec((None, 1, D_MODEL), lambda b, i: (b, 0, 0))


def _mod_kernel(cond_ref, w_ref, b_ref, o_ref):
    s = _silu(cond_ref[...]).astype(BF16)
    o_ref[...] = _dot(s, w_ref[...].astype(BF16)) + b_ref[...]


def _modulation(cond, w_mod, b_mod):
    n = cond.shape[0]
    return pl.pallas_call(
        _mod_kernel,
        grid=(DEPTH, 3),
        in_specs=[
            pl.BlockSpec((n, D_MODEL), lambda l, j: (0, 0)),
            pl.BlockSpec((None, D_MODEL, D_MODEL), lambda l, j: (l, 0, j)),
            pl.BlockSpec((None, 1, D_MODEL), lambda l, j: (l, 0, j)),
        ],
        out_specs=pl.BlockSpec((None, n, D_MODEL), lambda l, j: (l, 0, j)),
        out_shape=jax.ShapeDtypeStruct((DEPTH, n, 3 * D_MODEL), F32),
        compiler_params=_params("arbitrary", "arbitrary"),
        name="modulation",
    )(cond, w_mod, b_mod.reshape(DEPTH, 1, 3 * D_MODEL))


def _ab_in_kernel(x_ref, shift_ref, scale_ref, w_ref, lng_ref, lnb_ref, ws_ref, bs_ref,
                  lx_ref, ga_ref, yb_ref, *, tm):
    h = (x_ref[...] * (1.0 + scale_ref[...]) + shift_ref[...]).astype(BF16)

    def proj(j):
        return _dot(h, w_ref[:, j * W:(j + 1) * W])

    lx_ref[...] = proj(0)
    ga_ref[...] = _silu(proj(1)).astype(BF16)
    u = jax.nn.gelu(proj(2))
    v = jax.nn.gelu(proj(3))
    mu = jnp.mean(v, axis=-1, keepdims=True)
    vc = v - mu
    var = jnp.mean(vc * vc, axis=-1, keepdims=True)
    vn = ((vc * lax.rsqrt(var + LN_EPS)) * lng_ref[...] + lnb_ref[...]).astype(BF16)
    ug = u * _silu(proj(4))
    for c in range(tm // SGU_CHUNK):
        rows = slice(c * SGU_CHUNK, (c + 1) * SGU_CHUNK)
        for g in range(SGU_GROUPS):
            cols = slice(g * LANES, (g + 1) * LANES)
            s = _dot(ws_ref[g], vn[rows, cols]) + bs_ref[g]
            yb_ref[rows, cols] = (ug[rows, cols] * s).astype(BF16)


def _ab_in(x, shift, scale, w_in, sgu_ln_g, sgu_ln_b, sgu_w, sgu_b, tm):
    bsz, n, _ = x.shape
    tok = lambda width: pl.BlockSpec((None, tm, width), lambda b, i: (b, i, 0))
    consts = [w_in, sgu_ln_g, sgu_ln_b, sgu_w, sgu_b]
    return pl.pallas_call(
        functools.partial(_ab_in_kernel, tm=tm),
        grid=(bsz, n // tm),
        in_specs=[tok(D_MODEL), _mod_spec(shift), _mod_spec(scale)] + [_bcast_spec(a) for a in consts],
        out_specs=[tok(W), tok(W), tok(W)],
        out_shape=[jax.ShapeDtypeStruct((bsz, n, W), F32),
                   jax.ShapeDtypeStruct((bsz, n, W), BF16),
                   jax.ShapeDtypeStruct((bsz, n, W), BF16)],
        compiler_params=_params("parallel", "parallel"),
        name="ab_in_sgu",
    )(x, shift, scale, *consts)


SCAN_SEGMENTS = SUBLANES
SCAN_BLOCK = 32
SCAN_STATE_PAD = SUBLANES


def _lru_kernel(lx_ref, lxc_ref, ga_ref, gac_ref, cw_ref, cb_ref, wp_ref, ba_ref, bx_ref, lam_ref,
                ya_ref, yac_ref, xp_ref, xpc_ref, af_ref, bf_ref, ar_ref, br_ref, hf_ref, hr_ref, *, n, nc, tr):
    nt = n + nc
    lseg = nt // SCAN_SEGMENTS
    h_pitch = lseg + SCAN_STATE_PAD
    pad = SUBLANES
    wl = lx_ref.shape[-1]
    npair = wl // LANES
    zeros = jnp.zeros((pad, wl), F32)
    for ref, src, length in ((xp_ref, lx_ref, n), (xpc_ref, lxc_ref, nc)):
        ref[0:pad, :] = zeros
        ref[pad + length:2 * pad + length, :] = zeros
        ref[pad:pad + length, :] = src[...]

    lam = lam_ref[...]
    neg = -lam
    softplus = jnp.maximum(neg, 0.0) + jnp.log1p(jnp.exp(-jnp.abs(neg)))
    rate = LRU_C * softplus
    rate_log2 = -LOG2_E * rate
    cw = cw_ref[...]
    cb = cb_ref[...]

    def block_rows(pos):
        if isinstance(pos, int):
            seg, step = divmod(pos, lseg)
        else:
            seg = lax.div(pos, lseg)
            step = pos - seg * lseg
        return pl.ds(step * SCAN_SEGMENTS + seg, SCAN_BLOCK, stride=SCAN_SEGMENTS)

    def coeffs(src_ref, r0, f_off, r_off):
        xw = src_ref[r0:r0 + tr + 2 * pad, :]
        taps = [pltpu.roll(xw, (2 - k) % (tr + 2 * pad), 0)[pad:pad + tr, :] for k in range(4)]
        xc = cb + sum(cw[k:k + 1, :] * taps[k] for k in range(4))
        xcb = xc.astype(BF16)
        dst = [[block_rows(off + r0 + blk) for blk in range(0, tr, SCAN_BLOCK)] for off in (f_off, r_off)]
        for j in range(npair):
            cols = slice(j * LANES, (j + 1) * LANES)
            gts = _dot(xcb[:, cols], wp_ref[j])
            xj = xc[:, cols]
            for d, (a_ref, b_ref) in enumerate(((af_ref, bf_ref), (ar_ref, br_ref))):
                gate_r = jax.nn.sigmoid(gts[:, (2 * d) * LANES:(2 * d + 1) * LANES] + ba_ref[d:d + 1, cols])
                gate_i = jax.nn.sigmoid(gts[:, (2 * d + 1) * LANES:(2 * d + 2) * LANES] + bx_ref[d:d + 1, cols])
                a = jnp.exp2(gate_r * rate_log2[d:d + 1, cols])
                one_minus_a2 = jnp.tanh(gate_r * rate[d:d + 1, cols]) * (1.0 + a * a)
                root = jnp.where(one_minus_a2 > 0.0, one_minus_a2 * lax.rsqrt(one_minus_a2), 0.0)
                b = root * (gate_i * xj)
                for rows, blk in zip(dst[d], range(0, tr, SCAN_BLOCK)):
                    a_ref[j, rows, :] = a[blk:blk + SCAN_BLOCK, :]
                    b_ref[j, rows, :] = b[blk:blk + SCAN_BLOCK, :]

    for i in range(n // tr):
        coeffs(xp_ref, i * tr, nc, 0)
    for i in range(nc // tr):
        coeffs(xpc_ref, i * tr, 0, n)

    chains = [(a_ref, b_ref, h_ref, j, rev)
              for a_ref, b_ref, h_ref, rev in ((af_ref, bf_ref, hf_ref, False), (ar_ref, br_ref, hr_ref, True))
              for j in range(npair)]

    def step_rows(t, rev):
        return pl.ds(pl.multiple_of((lseg - 1 - t if rev else t) * SCAN_SEGMENTS, SCAN_SEGMENTS), SCAN_SEGMENTS)

    def totals_step(t, carry):
        out = []
        for (a_ref, b_ref, _, j, rev), (decay, h) in zip(chains, carry):
            rows = step_rows(t, rev)
            a = a_ref[j, rows, :]
            out.append((a * decay, a * h + b_ref[j, rows, :]))
        return tuple(out)

    one = jnp.ones((SCAN_SEGMENTS, LANES), F32)
    zero = jnp.zeros((SCAN_SEGMENTS, LANES), F32)
    totals = lax.fori_loop(0, lseg, totals_step, tuple((one, zero) for _ in chains), unroll=True)

    def entering_state(decay, h_end, rev):
        order = range(SCAN_SEGMENTS - 1, -1, -1) if rev else range(SCAN_SEGMENTS)
        rows = [None] * SCAN_SEGMENTS
        state = jnp.zeros((1, LANES), F32)
        for s in order:
            rows[s] = state
            state = decay[s:s + 1, :] * state + h_end[s:s + 1, :]
        return jnp.concatenate(rows, axis=0)

    starts = tuple(entering_state(decay, h_end, chain[-1]) for chain, (decay, h_end) in zip(chains, totals))

    def states_step(t, carry):
        out = []
        for (a_ref, b_ref, h_ref, j, rev), h in zip(chains, carry):
            rows = step_rows(t, rev)
            h = a_ref[j, rows, :] * h + b_ref[j, rows, :]
            h_ref[j, pl.ds(lseg - 1 - t if rev else t, SCAN_SEGMENTS, stride=h_pitch), :] = h
            out.append(h)
        return tuple(out)

    lax.fori_loop(0, lseg, states_step, starts, unroll=True)

    def state_rows(pos):
        if isinstance(pos, int):
            seg, step = divmod(pos, lseg)
            return pl.ds(seg * h_pitch + step, SCAN_BLOCK)
        seg = lax.div(pos, lseg)
        return pl.ds(pl.multiple_of(seg * h_pitch + (pos - seg * lseg), SUBLANES), SCAN_BLOCK)

    def write_out(y_ref, g_ref, r0, f_pos, r_pos, rows):
        for blk in range(0, rows, SCAN_BLOCK):
            dst = pl.ds(r0 + blk, SCAN_BLOCK)
            rows_f = state_rows(f_pos + blk)
            rows_r = state_rows(r_pos + blk)
            for j in range(npair):
                cols = slice(j * LANES, (j + 1) * LANES)
                h = hf_ref[j, rows_f, :] + hr_ref[j, rows_r, :]
                y_ref[dst, cols] = (h * g_ref[dst, cols].astype(F32)).astype(BF16)

    write_out(ya_ref, ga_ref, 0, nc, 0, n)
    write_out(yac_ref, gac_ref, 0, 0, n, nc)


def _lru(lx, lxc, ga, gac, conv_w, conv_b, wpair, b_a, b_x, lam):
    bsz, n, _ = lx.shape
    nc = lxc.shape[1]
    tr = LRU_ROWS
    lseg, rem = divmod(n + nc, SCAN_SEGMENTS)
    assert rem == 0 and lseg % SCAN_BLOCK == 0 and n % tr == 0 and nc % tr == 0 and tr % SCAN_BLOCK == 0
    npair = LRU_PAIRS_PER_STEP
    wl = npair * LANES
    seq = lambda length: pl.BlockSpec((None, length, wl), lambda b, g: (b, 0, g))
    vec = lambda rows: pl.BlockSpec((rows, wl), lambda b, g: (0, g))
    pad = 2 * SUBLANES
    coeff = pltpu.VMEM((npair, SCAN_SEGMENTS * lseg, LANES), F32)
    state = pltpu.VMEM((npair, SCAN_SEGMENTS * (lseg + SCAN_STATE_PAD), LANES), F32)
    return pl.pallas_call(
        functools.partial(_lru_kernel, n=n, nc=nc, tr=tr),
        grid=(bsz, N_PAIR // npair),
        in_specs=[seq(n), seq(nc), seq(n), seq(nc), vec(conv_w.shape[0]), vec(1),
                  pl.BlockSpec((npair,) + wpair.shape[1:], lambda b, g: (g, 0, 0)), vec(2), vec(2), vec(2)],
        out_specs=[seq(n), seq(nc)],
        out_shape=[jax.ShapeDtypeStruct((bsz, n, W), BF16), jax.ShapeDtypeStruct((bsz, nc, W), BF16)],
        scratch_shapes=[pltpu.VMEM((n + pad, wl), F32), pltpu.VMEM((nc + pad, wl), F32)] + [coeff] * 4 + [state] * 2,
        compiler_params=_params("parallel", "parallel"),
        name="rglru",
    )(lx, lxc, ga, gac, conv_w, conv_b, wpair, b_a, b_x, lam)


def _residual_ln(x, y, gate, g, b):
    z = DEEPNORM_ALPHA * x + gate * y
    mu = jnp.mean(z, axis=-1, keepdims=True)
    zc = z - mu
    var = jnp.mean(zc * zc, axis=-1, keepdims=True)
    return (zc * lax.rsqrt(var + LN_EPS)) * g + b


def _ctx_out_kv_kernel(x_ref, ya_ref, yb_ref, gate_ref, wo_ref, g_ref, b_ref, shift_ref, scale_ref, w_ref,
                       k_ref, v_ref):
    y = _dot(ya_ref[...], wo_ref[0:W, :]) + _dot(yb_ref[...], wo_ref[W:2 * W, :])
    c1 = _residual_ln(x_ref[...], y, gate_ref[...], g_ref[...], b_ref[...])
    h = (c1 * (1.0 + scale_ref[...]) + shift_ref[...]).astype(BF16)
    k_ref[...] = _dot(h, w_ref[:, 0:W]).astype(BF16)
    v_ref[...] = _dot(h, w_ref[:, W:2 * W]).astype(BF16)


def _ctx_out_kv(ctx, ya, yb, gate0, w_out, ln_g, ln_b, shift, scale, w_kv, tm):
    bsz, n, _ = ctx.shape
    tok = lambda width: pl.BlockSpec((None, tm, width), lambda b, i: (b, i, 0))
    return pl.pallas_call(
        _ctx_out_kv_kernel,
        grid=(bsz, n // tm),
        in_specs=[tok(D_MODEL), tok(W), tok(W), _mod_spec(gate0), _bcast_spec(w_out), _bcast_spec(ln_g),
                  _bcast_spec(ln_b), _mod_spec(shift), _mod_spec(scale), _bcast_spec(w_kv)],
        out_specs=[tok(W), tok(W)],
        out_shape=[jax.ShapeDtypeStruct((bsz, n, W), BF16)] * 2,
        compiler_params=_params("parallel", "parallel"),
        name="ctx_out_kv",
    )(ctx, ya, yb, gate0, w_out, ln_g, ln_b, shift, scale, w_kv)


def _rope(t, cos, sin_signed, first_half):
    parts = []
    for j in range(W // LANES):
        tj = t[:, j * LANES:(j + 1) * LANES]
        parts.append(jnp.where(first_half, pltpu.roll(tj, LANES - 16, 1), pltpu.roll(tj, 16, 1)))
    return t * cos + jnp.concatenate(parts, axis=1) * sin_signed


def _ab_out_cd_in_kernel(x_ref, ya_ref, yb_ref, gate_ref, wo_ref, g_ref, b_ref, shift_ref, scale_ref, w_ref,
                         cos_ref, sin_ref, x1_ref, z_ref, cg_ref, q_ref, k_ref, v_ref, ag_ref, *, tm):
    y = _dot(ya_ref[...], wo_ref[0:W, :]) + _dot(yb_ref[...], wo_ref[W:2 * W, :])
    x1 = _residual_ln(x_ref[...], y, gate_ref[...], g_ref[...], b_ref[...])
    x1_ref[...] = x1
    h = (x1 * (1.0 + scale_ref[...]) + shift_ref[...]).astype(BF16)

    def proj(j):
        return _dot(h, w_ref[:, j * W:(j + 1) * W])

    z_ref[...] = proj(2) * proj(0)
    cg_ref[...] = (proj(1) * _silu(proj(3))).astype(BF16)
    cos = cos_ref[...]
    sin = sin_ref[...]
    lane = lax.broadcasted_iota(jnp.int32, (tm, LANES), 1)
    first_half = (lane % 32) < 16
    q_ref[...] = (_rope(proj(4), cos, sin, first_half) * (DIFF_HEAD_DIM ** -0.5 * LOG2_E)).astype(BF16)
    k_ref[...] = _rope(proj(5), cos, sin, first_half).astype(BF16)
    v_ref[...] = proj(6).astype(BF16)
    ag_ref[...] = _silu(proj(7)).astype(BF16)


def _ab_out_cd_in(x, ya, yb, gate0, w_out, ln_g, ln_b, shift, scale, w_in, cos, sin_signed, tm):
    bsz, n, _ = x.shape
    tok = lambda width: pl.BlockSpec((None, tm, width), lambda i, b: (b, i, 0))
    mod = pl.BlockSpec((None, 1, D_MODEL), lambda i, b: (b, 0, 0))
    table = pl.BlockSpec((tm, W), lambda i, b: (i, 0))
    const = lambda a: pl.BlockSpec(a.shape, lambda i, b: (0,) * a.ndim, pipeline_mode=pl.Buffered(1))
    return pl.pallas_call(
        functools.partial(_ab_out_cd_in_kernel, tm=tm),
        grid=(n // tm, bsz),
        in_specs=[tok(D_MODEL), tok(W), tok(W), mod, const(w_out), const(ln_g), const(ln_b), mod, mod, const(w_in),
                  table, table],
        out_specs=[tok(D_MODEL)] + [tok(W)] * 6,
        out_shape=[jax.ShapeDtypeStruct((bsz, n, D_MODEL), F32), jax.ShapeDtypeStruct((bsz, n, W), F32)]
        + [jax.ShapeDtypeStruct((bsz, n, W), BF16)] * 5,
        compiler_params=_params("parallel", "parallel"),
        name="ab_out_cd_in",
    )(x, ya, yb, gate0, w_out, ln_g, ln_b, shift, scale, w_in, cos, sin_signed)


def _attn_out_kernel(q_ref, kc_ref, k_ref, vc_ref, v_ref, ag_ref, x_ref, z_ref, zprev_ref, znext_ref, cg_ref,
                     gate_ref, dl_ref, sg_ref, cw_ref, w_ref, g_ref, b_ref, o_ref,
                     kall_ref, vall_ref, qh_ref, yd_ref, s_ref, p_ref, al_ref, m_ref, acc_ref, *, tq, ck):
    nc = kc_ref.shape[0]
    n = k_ref.shape[0]
    nck = len(ck)
    starts = [sum(ck[:j]) for j in range(nck)]
    i = pl.program_id(1)

    @pl.when(i == 0)
    def _():
        for h in range(DIFF_HEADS):
            cols = slice(h * DIFF_V_DIM, (h + 1) * DIFF_V_DIM)
            kall_ref[h, 0:n, :] = k_ref[:, cols]
            kall_ref[h, n:n + nc, :] = kc_ref[:, cols]
            vall_ref[h, 0:n, 0:DIFF_V_DIM] = v_ref[:, cols]
            vall_ref[h, n:n + nc, 0:DIFF_V_DIM] = vc_ref[:, cols]
            vall_ref[h, :, DIFF_V_DIM:2 * DIFF_V_DIM] = jnp.ones((n + nc, DIFF_V_DIM), BF16)

    dl = dl_ref[...]
    lam = (jnp.exp(jnp.sum(dl[0:1, :] * dl[1:2, :], axis=-1, keepdims=True))
           - jnp.exp(jnp.sum(dl[2:3, :] * dl[3:4, :], axis=-1, keepdims=True)) + LAM_INIT_1)
    lane = lax.broadcasted_iota(jnp.int32, (tq, DIFF_V_DIM), 1)
    for h in range(DIFF_HEADS):
        q = q_ref[:, h * DIFF_V_DIM:(h + 1) * DIFF_V_DIM]
        zero = jnp.zeros_like(q)
        qh_ref[h, 0:tq, :] = jnp.where(lane < DIFF_HEAD_DIM, q, zero)
        qh_ref[h, tq:2 * tq, :] = jnp.where(lane >= DIFF_HEAD_DIM, q, zero)
    nt = (((1,), (1,)), ((), ()))
    slots = s_ref.shape[0]

    def head(h, carry):
        qq = qh_ref[h]

        def scores(j, slot):
            kj = kall_ref[h, starts[j]:starts[j] + ck[j], :]
            s_ref[slot, :, 0:ck[j]] = lax.dot_general(qq, kj, nt, preferred_element_type=F32)

        def softmax(j, slot, first):
            s = s_ref[slot, :, 0:ck[j]]
            sm = s[:, 0:LANES]
            for c in range(1, ck[j] // LANES):
                sm = jnp.maximum(sm, s[:, c * LANES:(c + 1) * LANES])
            m_new = jnp.broadcast_to(jnp.max(sm, axis=-1, keepdims=True), (2 * tq, LANES))
            if not first:
                m_old = m_ref[...]
                m_new = jnp.maximum(m_old, m_new)
                al_ref[slot] = jnp.exp2(m_old - m_new)
            m_ref[...] = m_new
            p_ref[slot, :, 0:ck[j]] = jnp.exp2(s - jnp.concatenate([m_new] * (ck[j] // LANES), axis=1)).astype(BF16)

        def weighted_values(j, slot, first):
            pv = _dot(p_ref[slot, :, 0:ck[j]], vall_ref[h, starts[j]:starts[j] + ck[j], :])
            if first:
                acc_ref[...] = pv
            else:
                alpha = al_ref[slot]
                acc_ref[...] = jnp.concatenate([alpha, alpha], axis=1) * acc_ref[...] + pv

        for t in range(nck + 2):
            if t < nck:
                scores(t, t % slots)
            if 1 <= t <= nck:
                softmax(t - 1, (t - 1) % slots, t == 1)
            if t >= 2:
                weighted_values(t - 2, (t - 2) % slots, t == 2)

        acc = acc_ref[:, 0:DIFF_V_DIM]
        w = 1.0 / acc_ref[:, DIFF_V_DIM:2 * DIFF_V_DIM]
        o = acc[0:tq, :] * w[0:tq, :] - acc[tq:2 * tq, :] * (lam * w[tq:2 * tq, :])
        yd_ref[h] = (o * lax.rsqrt(jnp.mean(o * o, axis=-1, keepdims=True) + LN_EPS)) * sg_ref[...] * (1.0 - LAM_INIT_1)
        return carry

    lax.fori_loop(0, DIFF_HEADS, head, 0, unroll=True)

    yd = jnp.concatenate([yd_ref[h] for h in range(DIFF_HEADS)], axis=1) * ag_ref[...].astype(F32)
    z = z_ref[...]
    row = lax.broadcasted_iota(jnp.int32, (tq, W), 0)
    prev_row = jnp.where(i > 0, zprev_ref[SUBLANES - 1:SUBLANES, :], 0.0)
    next_row = jnp.where(i < pl.num_programs(1) - 1, znext_ref[0:1, :], 0.0)
    z_m1 = jnp.where(row == 0, prev_row, pltpu.roll(z, 1, 0))
    z_p1 = jnp.where(row == tq - 1, next_row, pltpu.roll(z, tq - 1, 0))
    cw = cw_ref[...]
    conv = cw[0:1, :] * z_m1 + cw[1:2, :] * z + cw[2:3, :] * z_p1
    yc = (cg_ref[...].astype(F32) * conv).astype(BF16)
    y = _dot(yc, w_ref[0:W, :]) + _dot(yd.astype(BF16), w_ref[W:2 * W, :])
    o_ref[...] = _residual_ln(x_ref[...], y, gate_ref[...], g_ref[...], b_ref[...])


def _attention_out(q, kc, k, vc, v, ag, x, z, cg, gate, diff_lambda, subln_g, sconv_w, w_out, ln_g, ln_b, tq, ck):
    bsz, n, _ = q.shape
    nc = kc.shape[1]
    assert sum(ck) == n + nc and all(c % (2 * LANES) == 0 for c in ck)
    tok = lambda width: pl.BlockSpec((None, tq, width), lambda b, i: (b, i, 0))
    keys = lambda length: pl.BlockSpec((None, length, W), lambda b, i: (b, 0, 0))
    per = tq // SUBLANES
    last = n // SUBLANES - 1
    zprev = pl.BlockSpec((None, SUBLANES, W), lambda b, i: (b, jnp.maximum(i * per - 1, 0), 0))
    znext = pl.BlockSpec((None, SUBLANES, W), lambda b, i: (b, jnp.minimum((i + 1) * per, last), 0))
    const = lambda a: pl.BlockSpec(a.shape, lambda b, i: (0,) * a.ndim, pipeline_mode=pl.Buffered(1))
    consts = [diff_lambda, subln_g, sconv_w, w_out, ln_g, ln_b]
    slots = 2
    return pl.pallas_call(
        functools.partial(_attn_out_kernel, tq=tq, ck=ck),
        grid=(bsz, n // tq),
        in_specs=[tok(W), keys(nc), keys(n), keys(nc), keys(n), tok(W), tok(D_MODEL), tok(W), zprev, znext, tok(W),
                  _mod_spec(gate)] + [const(a) for a in consts],
        out_specs=tok(D_MODEL),
        out_shape=jax.ShapeDtypeStruct((bsz, n, D_MODEL), F32),
        scratch_shapes=[pltpu.VMEM((DIFF_HEADS, n + nc, DIFF_V_DIM), BF16),
                        pltpu.VMEM((DIFF_HEADS, n + nc, 2 * DIFF_V_DIM), BF16),
                        pltpu.VMEM((DIFF_HEADS, 2 * tq, DIFF_V_DIM), BF16),
                        pltpu.VMEM((DIFF_HEADS, tq, DIFF_V_DIM), F32),
                        pltpu.VMEM((slots, 2 * tq, max(ck)), F32), pltpu.VMEM((slots, 2 * tq, max(ck)), BF16),
                        pltpu.VMEM((slots, 2 * tq, LANES), F32), pltpu.VMEM((2 * tq, LANES), F32),
                        pltpu.VMEM((2 * tq, 2 * DIFF_V_DIM), F32)],
        compiler_params=_params("parallel", "arbitrary"),
        name="attention_out",
    )(q, kc, k, vc, v, ag, x, z, z, z, cg, gate, *consts)


def _pair_gate_weights(w_a, w_x):
    def pair_blockdiag(w):
        wp = w.reshape(N_PAIR, 2, LRU_HEAD_DIM, LRU_HEAD_DIM)
        zero = jnp.zeros_like(wp[:, 0])
        top = jnp.concatenate([wp[:, 0], zero], axis=-1)
        bot = jnp.concatenate([zero, wp[:, 1]], axis=-1)
        return jnp.concatenate([top, bot], axis=-2)
    return jnp.concatenate([pair_blockdiag(w_a[0]), pair_blockdiag(w_x[0]),
                            pair_blockdiag(w_a[1]), pair_blockdiag(w_x[1])], axis=-1).astype(BF16)


def _rope_tables(n):
    rows = n // GRID_W
    row = jnp.repeat(jnp.arange(rows, dtype=F32), GRID_W)
    col = jnp.tile(jnp.arange(GRID_W, dtype=F32), rows)
    n_freq = DIFF_HEAD_DIM // 4
    inv = ROPE_BASE ** (-jnp.arange(n_freq, dtype=F32) / n_freq)
    ang_r = row[:, None] * inv
    ang_c = col[:, None] * inv
    ang = jnp.concatenate([ang_r, ang_r, ang_c, ang_c], axis=-1)
    sign = jnp.tile(jnp.concatenate([-jnp.ones((n_freq,), F32), jnp.ones((n_freq,), F32)]), 2)
    reps = W // DIFF_HEAD_DIM
    return jnp.tile(jnp.cos(ang), (1, reps)), jnp.tile(jnp.sin(ang) * sign, (1, reps))


def kernel(x, c, ctx, c_ctx, w_mod, b_mod, ln_g, ln_b, ab_w_in, ab_w_out, lru_conv_w, lru_conv_b, lru_w_a, lru_b_a,
           lru_w_x, lru_b_x, lru_lambda, sgu_ln_g, sgu_ln_b, sgu_w, sgu_b, cd_w_in, cd_w_out, sconv_w, diff_lambda,
           diff_subln_g):
    bsz, n, _ = x.shape
    nc = ctx.shape[1]
    tiles = _tiling(n, nc)
    row = lambda a: a.reshape(1, -1)

    n_cond = -(-(bsz + 1) // SUBLANES) * SUBLANES
    cond = jnp.zeros((n_cond, D_MODEL), F32).at[:bsz].set(c).at[bsz].set(c_ctx)
    mod = _modulation(cond, w_mod, b_mod)

    def mods(l):
        lat = [mod[l, :bsz, j * D_MODEL:(j + 1) * D_MODEL][:, None, :] for j in range(3)]
        con = [mod[l, bsz:bsz + 1, j * D_MODEL:(j + 1) * D_MODEL][:, None, :] for j in range(3)]
        return lat, con

    (shift, scale, gate), (shift_c, scale_c, gate_c) = mods(0)
    w_in = ab_w_in[0].astype(BF16)
    w_out = ab_w_out[0].astype(BF16)
    sgu_args = (row(sgu_ln_g[0]), row(sgu_ln_b[0]), sgu_w[0].astype(BF16),
                jnp.broadcast_to(sgu_b[0][:, :, None], (SGU_GROUPS, SGU_CHUNK, LANES)))
    lx, ga, yb = _ab_in(x, shift, scale, w_in, *sgu_args, tm=tiles.proj_rows)
    lxc, gac, ybc = _ab_in(ctx, shift_c, scale_c, w_in, *sgu_args, tm=tiles.ctx_rows)
    wpair = _pair_gate_weights(lru_w_a[0], lru_w_x[0])
    ya, yac = _lru(lx, lxc, ga, gac, lru_conv_w[0], row(lru_conv_b[0]), wpair, lru_b_a[0], lru_b_x[0], lru_lambda[0])

    (shift, scale, gate1), (shift_c, scale_c, _) = mods(1)
    w_in = cd_w_in[0].astype(BF16)
    cos, sin_signed = _rope_tables(n)
    x1, z, cg, q, k, v, ag = _ab_out_cd_in(x, ya, yb, gate, w_out, row(ln_g[0]), row(ln_b[0]), shift, scale, w_in,
                                           cos, sin_signed, tm=tiles.proj_rows)
    kc, vc = _ctx_out_kv(ctx, yac, ybc, gate_c, w_out, row(ln_g[0]), row(ln_b[0]), shift_c, scale_c,
                         w_in[:, 5 * W:7 * W], tm=tiles.ctx_rows)
    gate = gate1
    w_out = cd_w_out[0].astype(BF16)
    return _attention_out(q, kc, k, vc, v, ag, x1, z, cg, gate, diff_lambda[0], row(diff_subln_g[0]), sconv_w[0], w_out,
                          row(ln_g[1]), row(ln_b[1]), tq=tiles.query_rows, ck=tiles.key_chunks)
```

```python
import functools
import math
from typing import NamedTuple

import jax
import jax.numpy as jnp
from jax import lax
from jax.experimental import pallas as pl
from jax.experimental.pallas import tpu as pltpu

D_MODEL = 1024
DEPTH = 2
GRID_W = 64
W = D_MODEL // 2
LRU_HEAD_DIM = 64
LRU_C = 8.0
LRU_PAIR = 2 * LRU_HEAD_DIM
N_PAIR = W // LRU_PAIR
SGU_CHUNK = 128
SGU_GROUPS = 4
DIFF_V_DIM = 128
DIFF_HEADS = W // DIFF_V_DIM
DIFF_HEAD_DIM = DIFF_V_DIM // 2
ROPE_BASE = 10000.0
DEEPNORM_ALPHA = (2 * DEPTH) ** 0.25
LN_EPS = 1e-5
LAM_INIT_1 = 0.8 - 0.6 * math.exp(-0.3 * 1)
LOG2_E = math.log2(math.e)

SUBLANES = 8
LANES = 128
VMEM_LIMIT = 56 * 1024 * 1024

F32 = jnp.float32
BF16 = jnp.bfloat16

MXU_DIM = 256
PROJ_ROWS = 2 * MXU_DIM
KEY_CHUNK = 4 * MXU_DIM
LRU_ROWS = MXU_DIM
LRU_PAIRS_PER_STEP = 4


class _Tiling(NamedTuple):
    proj_rows: int
    ctx_rows: int
    query_rows: int
    key_chunks: tuple


def _tiling(n, nc):
    assert n % PROJ_ROWS == 0 and n % KEY_CHUNK == 0 and nc % MXU_DIM == 0 and nc <= KEY_CHUNK
    return _Tiling(PROJ_ROWS, nc, PROJ_ROWS, (KEY_CHUNK,) * (n // KEY_CHUNK) + (nc,))


def _silu(x):
    return x * jax.nn.sigmoid(x)


def _dot(a, b):
    return jnp.dot(a, b, preferred_element_type=F32)


def _params(*sem):
    return pltpu.CompilerParams(dimension_semantics=sem, vmem_limit_bytes=VMEM_LIMIT)


def _bcast_spec(arr):
    zeros = (0,) * arr.ndim
    return pl.BlockSpec(arr.shape, lambda *_: zeros)


def _mod_spec(arr):
    if arr.shape[0] == 1:
        return pl.BlockSpec((None, 1, D_MODEL), lambda b, i: (0, 0, 0))
    return pl.BlockSpec((None, 1, D_MODEL), lambda b, i: (b, 0, 0))


def _mod_kernel(cond_ref, w_ref, b_ref, o_ref):
    s = _silu(cond_ref[...]).astype(BF16)
    o_ref[...] = _dot(s, w_ref[...].astype(BF16)) + b_ref[...]


def _modulation(cond, w_mod, b_mod):
    n = cond.shape[0]
    return pl.pallas_call(
        _mod_kernel,
        grid=(DEPTH, 3),
        in_specs=[
            pl.BlockSpec((n, D_MODEL), lambda l, j: (0, 0)),
            pl.BlockSpec((None, D_MODEL, D_MODEL), lambda l, j: (l, 0, j)),
            pl.BlockSpec((None, 1, D_MODEL), lambda l, j: (l, 0, j)),
        ],
        out_specs=pl.BlockSpec((None, n, D_MODEL), lambda l, j: (l, 0, j)),
        out_shape=jax.ShapeDtypeStruct((DEPTH, n, 3 * D_MODEL), F32),
        compiler_params=_params("arbitrary", "arbitrary"),
        name="modulation",
    )(cond, w_mod, b_mod.reshape(DEPTH, 1, 3 * D_MODEL))


def _ab_in_kernel(x_ref, shift_ref, scale_ref, w_ref, lng_ref, lnb_ref, ws_ref, bs_ref,
                  lx_ref, ga_ref, yb_ref, *, tm):
    h = (x_ref[...] * (1.0 + scale_ref[...]) + shift_ref[...]).astype(BF16)

    def proj(j):
        return _dot(h, w_ref[:, j * W:(j + 1) * W])

    lx_ref[...] = proj(0)
    ga_ref[...] = _silu(proj(1)).astype(BF16)
    u = jax.nn.gelu(proj(2))
    v = jax.nn.gelu(proj(3))
    mu = jnp.mean(v, axis=-1, keepdims=True)
    vc = v - mu
    var = jnp.mean(vc * vc, axis=-1, keepdims=True)
    vn = ((vc * lax.rsqrt(var + LN_EPS)) * lng_ref[...] + lnb_ref[...]).astype(BF16)
    ug = u * _silu(proj(4))
    for c in range(tm // SGU_CHUNK):
        rows = slice(c * SGU_CHUNK, (c + 1) * SGU_CHUNK)
        for g in range(SGU_GROUPS):
            cols = slice(g * LANES, (g + 1) * LANES)
            s = _dot(ws_ref[g], vn[rows, cols]) + bs_ref[g]
            yb_ref[rows, cols] = (ug[rows, cols] * s).astype(BF16)


def _ab_in(x, shift, scale, w_in, sgu_ln_g, sgu_ln_b, sgu_w, sgu_b, tm):
    bsz, n, _ = x.shape
    tok = lambda width: pl.BlockSpec((None, tm, width), lambda b, i: (b, i, 0))
    consts = [w_in, sgu_ln_g, sgu_ln_b, sgu_w, sgu_b]
    return pl.pallas_call(
        functools.partial(_ab_in_kernel, tm=tm),
        grid=(bsz, n // tm),
        in_specs=[tok(D_MODEL), _mod_spec(shift), _mod_spec(scale)] + [_bcast_spec(a) for a in consts],
        out_specs=[tok(W), tok(W), tok(W)],
        out_shape=[jax.ShapeDtypeStruct((bsz, n, W), F32),
                   jax.ShapeDtypeStruct((bsz, n, W), BF16),
                   jax.ShapeDtypeStruct((bsz, n, W), BF16)],
        compiler_params=_params("parallel", "parallel"),
        name="ab_in_sgu",
    )(x, shift, scale, *consts)


SCAN_SEGMENTS = SUBLANES
SCAN_BLOCK = 32
SCAN_STATE_PAD = SUBLANES


def _lru_kernel(lx_ref, lxc_ref, ga_ref, gac_ref, cw_ref, cb_ref, wp_ref, ba_ref, bx_ref, lam_ref,
                ya_ref, yac_ref, xp_ref, xpc_ref, af_ref, bf_ref, ar_ref, br_ref, hf_ref, hr_ref, *, n, nc, tr):
    nt = n + nc
    lseg = nt // SCAN_SEGMENTS
    h_pitch = lseg + SCAN_STATE_PAD
    pad = SUBLANES
    wl = lx_ref.shape[-1]
    npair = wl // LANES
    zeros = jnp.zeros((pad, wl), F32)
    for ref, src, length in ((xp_ref, lx_ref, n), (xpc_ref, lxc_ref, nc)):
        ref[0:pad, :] = zeros
        ref[pad + length:2 * pad + length, :] = zeros
        ref[pad:pad + length, :] = src[...]

    lam = lam_ref[...]
    neg = -lam
    softplus = jnp.maximum(neg, 0.0) + jnp.log1p(jnp.exp(-jnp.abs(neg)))
    rate = LRU_C * softplus
    rate_log2 = -LOG2_E * rate
    cw = cw_ref[...]
    cb = cb_ref[...]

    def block_rows(pos):
        if isinstance(pos, int):
            seg, step = divmod(pos, lseg)
        else:
            seg = lax.div(pos, lseg)
            step = pos - seg * lseg
        return pl.ds(step * SCAN_SEGMENTS + seg, SCAN_BLOCK, stride=SCAN_SEGMENTS)

    def coeffs(src_ref, r0, f_off, r_off):
        xw = src_ref[r0:r0 + tr + 2 * pad, :]
        taps = [pltpu.roll(xw, (2 - k) % (tr + 2 * pad), 0)[pad:pad + tr, :] for k in range(4)]
        xc = cb + sum(cw[k:k + 1, :] * taps[k] for k in range(4))
        xcb = xc.astype(BF16)
        dst = [[block_rows(off + r0 + blk) for blk in range(0, tr, SCAN_BLOCK)] for off in (f_off, r_off)]
        for j in range(npair):
            cols = slice(j * LANES, (j + 1) * LANES)
            gts = _dot(xcb[:, cols], wp_ref[j])
            xj = xc[:, cols]
            for d, (a_ref, b_ref) in enumerate(((af_ref, bf_ref), (ar_ref, br_ref))):
                gate_r = jax.nn.sigmoid(gts[:, (2 * d) * LANES:(2 * d + 1) * LANES] + ba_ref[d:d + 1, cols])
                gate_i = jax.nn.sigmoid(gts[:, (2 * d + 1) * LANES:(2 * d + 2) * LANES] + bx_ref[d:d + 1, cols])
                a = jnp.exp2(gate_r * rate_log2[d:d + 1, cols])
                one_minus_a2 = jnp.tanh(gate_r * rate[d:d + 1, cols]) * (1.0 + a * a)
                root = jnp.where(one_minus_a2 > 0.0, one_minus_a2 * lax.rsqrt(one_minus_a2), 0.0)
                b = root * (gate_i * xj)
                for rows, blk in zip(dst[d], range(0, tr, SCAN_BLOCK)):
                    a_ref[j, rows, :] = a[blk:blk + SCAN_BLOCK, :]
                    b_ref[j, rows, :] = b[blk:blk + SCAN_BLOCK, :]

    for i in range(n // tr):
        coeffs(xp_ref, i * tr, nc, 0)
    for i in range(nc // tr):
        coeffs(xpc_ref, i * tr, 0, n)

    chains = [(a_ref, b_ref, h_ref, j, rev)
              for a_ref, b_ref, h_ref, rev in ((af_ref, bf_ref, hf_ref, False), (ar_ref, br_ref, hr_ref, True))
              for j in range(npair)]

    def step_rows(t, rev):
        return pl.ds(pl.multiple_of((lseg - 1 - t if rev else t) * SCAN_SEGMENTS, SCAN_SEGMENTS), SCAN_SEGMENTS)

    def totals_step(t, carry):
        out = []
        for (a_ref, b_ref, _, j, rev), (decay, h) in zip(chains, carry):
            rows = step_rows(t, rev)
            a = a_ref[j, rows, :]
            out.append((a * decay, a * h + b_ref[j, rows, :]))
        return tuple(out)

    one = jnp.ones((SCAN_SEGMENTS, LANES), F32)
    zero = jnp.zeros((SCAN_SEGMENTS, LANES), F32)
    totals = lax.fori_loop(0, lseg, totals_step, tuple((one, zero) for _ in chains), unroll=True)

    def entering_state(decay, h_end, rev):
        order = range(SCAN_SEGMENTS - 1, -1, -1) if rev else range(SCAN_SEGMENTS)
        rows = [None] * SCAN_SEGMENTS
        state = jnp.zeros((1, LANES), F32)
        for s in order:
            rows[s] = state
            state = decay[s:s + 1, :] * state + h_end[s:s + 1, :]
        return jnp.concatenate(rows, axis=0)

    starts = tuple(entering_state(decay, h_end, chain[-1]) for chain, (decay, h_end) in zip(chains, totals))

    def states_step(t, carry):
        out = []
        for (a_ref, b_ref, h_ref, j, rev), h in zip(chains, carry):
            rows = step_rows(t, rev)
            h = a_ref[j, rows, :] * h + b_ref[j, rows, :]
            h_ref[j, pl.ds(lseg - 1 - t if rev else t, SCAN_SEGMENTS, stride=h_pitch), :] = h
            out.append(h)
        return tuple(out)

    lax.fori_loop(0, lseg, states_step, starts, unroll=True)

    def state_rows(pos):
        if isinstance(pos, int):
            seg, step = divmod(pos, lseg)
            return pl.ds(seg * h_pitch + step, SCAN_BLOCK)
        seg = lax.div(pos, lseg)
        return pl.ds(pl.multiple_of(seg * h_pitch + (pos - seg * lseg), SUBLANES), SCAN_BLOCK)

    def write_out(y_ref, g_ref, r0, f_pos, r_pos, rows):
        for blk in range(0, rows, SCAN_BLOCK):
            dst = pl.ds(r0 + blk, SCAN_BLOCK)
            rows_f = state_rows(f_pos + blk)
            rows_r = state_rows(r_pos + blk)
            for j in range(npair):
                cols = slice(j * LANES, (j + 1) * LANES)
                h = hf_ref[j, rows_f, :] + hr_ref[j, rows_r, :]
                y_ref[dst, cols] = (h * g_ref[dst, cols].astype(F32)).astype(BF16)

    write_out(ya_ref, ga_ref, 0, nc, 0, n)
    write_out(yac_ref, gac_ref, 0, 0, n, nc)


def _lru(lx, lxc, ga, gac, conv_w, conv_b, wpair, b_a, b_x, lam):
    bsz, n, _ = lx.shape
    nc = lxc.shape[1]
    tr = LRU_ROWS
    lseg, rem = divmod(n + nc, SCAN_SEGMENTS)
    assert rem == 0 and lseg % SCAN_BLOCK == 0 and n % tr == 0 and nc % tr == 0 and tr % SCAN_BLOCK == 0
    npair = LRU_PAIRS_PER_STEP
    wl = npair * LANES
    seq = lambda length: pl.BlockSpec((None, length, wl), lambda b, g: (b, 0, g))
    vec = lambda rows: pl.BlockSpec((rows, wl), lambda b, g: (0, g))
    pad = 2 * SUBLANES
    coeff = pltpu.VMEM((npair, SCAN_SEGMENTS * lseg, LANES), F32)
    state = pltpu.VMEM((npair, SCAN_SEGMENTS * (lseg + SCAN_STATE_PAD), LANES), F32)
    return pl.pallas_call(
        functools.partial(_lru_kernel, n=n, nc=nc, tr=tr),
        grid=(bsz, N_PAIR // npair),
        in_specs=[seq(n), seq(nc), seq(n), seq(nc), vec(conv_w.shape[0]), vec(1),
                  pl.BlockSpec((npair,) + wpair.shape[1:], lambda b, g: (g, 0, 0)), vec(2), vec(2), vec(2)],
        out_specs=[seq(n), seq(nc)],
        out_shape=[jax.ShapeDtypeStruct((bsz, n, W), BF16), jax.ShapeDtypeStruct((bsz, nc, W), BF16)],
        scratch_shapes=[pltpu.VMEM((n + pad, wl), F32), pltpu.VMEM((nc + pad, wl), F32)] + [coeff] * 4 + [state] * 2,
        compiler_params=_params("parallel", "parallel"),
        name="rglru",
    )(lx, lxc, ga, gac, conv_w, conv_b, wpair, b_a, b_x, lam)


def _residual_ln(x, y, gate, g, b):
    z = DEEPNORM_ALPHA * x + gate * y
    mu = jnp.mean(z, axis=-1, keepdims=True)
    zc = z - mu
    var = jnp.mean(zc * zc, axis=-1, keepdims=True)
    return (zc * lax.rsqrt(var + LN_EPS)) * g + b


def _ctx_out_kv_kernel(x_ref, ya_ref, yb_ref, gate_ref, wo_ref, g_ref, b_ref, shift_ref, scale_ref, w_ref,
                       k_ref, v_ref):
    y = _dot(ya_ref[...], wo_ref[0:W, :]) + _dot(yb_ref[...], wo_ref[W:2 * W, :])
    c1 = _residual_ln(x_ref[...], y, gate_ref[...], g_ref[...], b_ref[...])
    h = (c1 * (1.0 + scale_ref[...]) + shift_ref[...]).astype(BF16)
    k_ref[...] = _dot(h, w_ref[:, 0:W]).astype(BF16)
    v_ref[...] = _dot(h, w_ref[:, W:2 * W]).astype(BF16)


def _ctx_out_kv(ctx, ya, yb, gate0, w_out, ln_g, ln_b, shift, scale, w_kv, tm):
    bsz, n, _ = ctx.shape
    tok = lambda width: pl.BlockSpec((None, tm, width), lambda b, i: (b, i, 0))
    return pl.pallas_call(
        _ctx_out_kv_kernel,
        grid=(bsz, n // tm),
        in_specs=[tok(D_MODEL), tok(W), tok(W), _mod_spec(gate0), _bcast_spec(w_out), _bcast_spec(ln_g),
                  _bcast_spec(ln_b), _mod_spec(shift), _mod_spec(scale), _bcast_spec(w_kv)],
        out_specs=[tok(W), tok(W)],
        out_shape=[jax.ShapeDtypeStruct((bsz, n, W), BF16)] * 2,
        compiler_params=_params("parallel", "parallel"),
        name="ctx_out_kv",
    )(ctx, ya, yb, gate0, w_out, ln_g, ln_b, shift, scale, w_kv)


def _rope(t, cos, sin_signed, first_half):
    parts = []
    for j in range(W // LANES):
        tj = t[:, j * LANES:(j + 1) * LANES]
        parts.append(jnp.where(first_half, pltpu.roll(tj, LANES - 16, 1), pltpu.roll(tj, 16, 1)))
    return t * cos + jnp.concatenate(parts, axis=1) * sin_signed


def _ab_out_cd_in_kernel(x_ref, ya_ref, yb_ref, gate_ref, wo_ref, g_ref, b_ref, shift_ref, scale_ref, w_ref,
                         cos_ref, sin_ref, x1_ref, z_ref, cg_ref, q_ref, k_ref, v_ref, ag_ref, *, tm):
    y = _dot(ya_ref[...], wo_ref[0:W, :]) + _dot(yb_ref[...], wo_ref[W:2 * W, :])
    x1 = _residual_ln(x_ref[...], y, gate_ref[...], g_ref[...], b_ref[...])
    x1_ref[...] = x1
    h = (x1 * (1.0 + scale_ref[...]) + shift_ref[...]).astype(BF16)

    def proj(j):
        return _dot(h, w_ref[:, j * W:(j + 1) * W])

    z_ref[...] = proj(2) * proj(0)
    cg_ref[...] = (proj(1) * _silu(proj(3))).astype(BF16)
    cos = cos_ref[...]
    sin = sin_ref[...]
    lane = lax.broadcasted_iota(jnp.int32, (tm, LANES), 1)
    first_half = (lane % 32) < 16
    q_ref[...] = (_rope(proj(4), cos, sin, first_half) * (DIFF_HEAD_DIM ** -0.5 * LOG2_E)).astype(BF16)
    k_ref[...] = _rope(proj(5), cos, sin, first_half).astype(BF16)
    v_ref[...] = proj(6).astype(BF16)
    ag_ref[...] = _silu(proj(7)).astype(BF16)


def _ab_out_cd_in(x, ya, yb, gate0, w_out, ln_g, ln_b, shift, scale, w_in, cos, sin_signed, tm):
    bsz, n, _ = x.shape
    tok = lambda width: pl.BlockSpec((None, tm, width), lambda i, b: (b, i, 0))
    mod = pl.BlockSpec((None, 1, D_MODEL), lambda i, b: (b, 0, 0))
    table = pl.BlockSpec((tm, W), lambda i, b: (i, 0))
    const = lambda a: pl.BlockSpec(a.shape, lambda i, b: (0,) * a.ndim, pipeline_mode=pl.Buffered(1))
    return pl.pallas_call(
        functools.partial(_ab_out_cd_in_kernel, tm=tm),
        grid=(n // tm, bsz),
        in_specs=[tok(D_MODEL), tok(W), tok(W), mod, const(w_out), const(ln_g), const(ln_b), mod, mod, const(w_in),
                  table, table],
        out_specs=[tok(D_MODEL)] + [tok(W)] * 6,
        out_shape=[jax.ShapeDtypeStruct((bsz, n, D_MODEL), F32), jax.ShapeDtypeStruct((bsz, n, W), F32)]
        + [jax.ShapeDtypeStruct((bsz, n, W), BF16)] * 5,
        compiler_params=_params("parallel", "parallel"),
        name="ab_out_cd_in",
    )(x, ya, yb, gate0, w_out, ln_g, ln_b, shift, scale, w_in, cos, sin_signed)


def _attn_out_kernel(q_ref, kc_ref, k_ref, vc_ref, v_ref, ag_ref, x_ref, z_ref, zprev_ref, znext_ref, cg_ref,
                     gate_ref, dl_ref, sg_ref, cw_ref, w_ref, g_ref, b_ref, o_ref,
                     kall_ref, vall_ref, qh_ref, yd_ref, s_ref, p_ref, al_ref, m_ref, acc_ref, *, tq, ck):
    nc = kc_ref.shape[0]
    n = k_ref.shape[0]
    nck = len(ck)
    starts = [sum(ck[:j]) for j in range(nck)]
    i = pl.program_id(1)

    @pl.when(i == 0)
    def _():
        for h in range(DIFF_HEADS):
            cols = slice(h * DIFF_V_DIM, (h + 1) * DIFF_V_DIM)
            kall_ref[h, 0:n, :] = k_ref[:, cols]
            kall_ref[h, n:n + nc, :] = kc_ref[:, cols]
            vall_ref[h, 0:n, 0:DIFF_V_DIM] = v_ref[:, cols]
            vall_ref[h, n:n + nc, 0:DIFF_V_DIM] = vc_ref[:, cols]
            vall_ref[h, :, DIFF_V_DIM:2 * DIFF_V_DIM] = jnp.ones((n + nc, DIFF_V_DIM), BF16)

    dl = dl_ref[...]
    lam = (jnp.exp(jnp.sum(dl[0:1, :] * dl[1:2, :], axis=-1, keepdims=True))
           - jnp.exp(jnp.sum(dl[2:3, :] * dl[3:4, :], axis=-1, keepdims=True)) + LAM_INIT_1)
    lane = lax.broadcasted_iota(jnp.int32, (tq, DIFF_V_DIM), 1)
    for h in range(DIFF_HEADS):
        q = q_ref[:, h * DIFF_V_DIM:(h + 1) * DIFF_V_DIM]
        zero = jnp.zeros_like(q)
        qh_ref[h, 0:tq, :] = jnp.where(lane < DIFF_HEAD_DIM, q, zero)
        qh_ref[h, tq:2 * tq, :] = jnp.where(lane >= DIFF_HEAD_DIM, q, zero)
    nt = (((1,), (1,)), ((), ()))
    slots = s_ref.shape[0]

    def head(h, carry):
        qq = qh_ref[h]

        def scores(j, slot):
            kj = kall_ref[h, starts[j]:starts[j] + ck[j], :]
            s_ref[slot, :, 0:ck[j]] = lax.dot_general(qq, kj, nt, preferred_element_type=F32)

        def softmax(j, slot, first):
            s = s_ref[slot, :, 0:ck[j]]
            sm = s[:, 0:LANES]
            for c in range(1, ck[j] // LANES):
                sm = jnp.maximum(sm, s[:, c * LANES:(c + 1) * LANES])
            m_new = jnp.broadcast_to(jnp.max(sm, axis=-1, keepdims=True), (2 * tq, LANES))
            if not first:
                m_old = m_ref[...]
                m_new = jnp.maximum(m_old, m_new)
                al_ref[slot] = jnp.exp2(m_old - m_new)
            m_ref[...] = m_new
            p_ref[slot, :, 0:ck[j]] = jnp.exp2(s - jnp.concatenate([m_new] * (ck[j] // LANES), axis=1)).astype(BF16)

        def weighted_values(j, slot, first):
            pv = _dot(p_ref[slot, :, 0:ck[j]], vall_ref[h, starts[j]:starts[j] + ck[j], :])
            if first:
                acc_ref[...] = pv
            else:
                alpha = al_ref[slot]
                acc_ref[...] = jnp.concatenate([alpha, alpha], axis=1) * acc_ref[...] + pv

        for t in range(nck + 2):
            if t < nck:
                scores(t, t % slots)
            if 1 <= t <= nck:
                softmax(t - 1, (t - 1) % slots, t == 1)
            if t >= 2:
                weighted_values(t - 2, (t - 2) % slots, t == 2)

        acc = acc_ref[:, 0:DIFF_V_DIM]
        w = 1.0 / acc_ref[:, DIFF_V_DIM:2 * DIFF_V_DIM]
        o = acc[0:tq, :] * w[0:tq, :] - acc[tq:2 * tq, :] * (lam * w[tq:2 * tq, :])
        yd_ref[h] = (o * lax.rsqrt(jnp.mean(o * o, axis=-1, keepdims=True) + LN_EPS)) * sg_ref[...] * (1.0 - LAM_INIT_1)
        return carry

    lax.fori_loop(0, DIFF_HEADS, head, 0, unroll=True)

    yd = jnp.concatenate([yd_ref[h] for h in range(DIFF_HEADS)], axis=1) * ag_ref[...].astype(F32)
    z = z_ref[...]
    row = lax.broadcasted_iota(jnp.int32, (tq, W), 0)
    prev_row = jnp.where(i > 0, zprev_ref[SUBLANES - 1:SUBLANES, :], 0.0)
    next_row = jnp.where(i < pl.num_programs(1) - 1, znext_ref[0:1, :], 0.0)
    z_m1 = jnp.where(row == 0, prev_row, pltpu.roll(z, 1, 0))
    z_p1 = jnp.where(row == tq - 1, next_row, pltpu.roll(z, tq - 1, 0))
    cw = cw_ref[...]
    conv = cw[0:1, :] * z_m1 + cw[1:2, :] * z + cw[2:3, :] * z_p1
    yc = (cg_ref[...].astype(F32) * conv).astype(BF16)
    y = _dot(yc, w_ref[0:W, :]) + _dot(yd.astype(BF16), w_ref[W:2 * W, :])
    o_ref[...] = _residual_ln(x_ref[...], y, gate_ref[...], g_ref[...], b_ref[...])


def _attention_out(q, kc, k, vc, v, ag, x, z, cg, gate, diff_lambda, subln_g, sconv_w, w_out, ln_g, ln_b, tq, ck):
    bsz, n, _ = q.shape
    nc = kc.shape[1]
    assert sum(ck) == n + nc and all(c % (2 * LANES) == 0 for c in ck)
    tok = lambda width: pl.BlockSpec((None, tq, width), lambda b, i: (b, i, 0))
    keys = lambda length: pl.BlockSpec((None, length, W), lambda b, i: (b, 0, 0))
    per = tq // SUBLANES
    last = n // SUBLANES - 1
    zprev = pl.BlockSpec((None, SUBLANES, W), lambda b, i: (b, jnp.maximum(i * per - 1, 0), 0))
    znext = pl.BlockSpec((None, SUBLANES, W), lambda b, i: (b, jnp.minimum((i + 1) * per, last), 0))
    const = lambda a: pl.BlockSpec(a.shape, lambda b, i: (0,) * a.ndim, pipeline_mode=pl.Buffered(1))
    consts = [diff_lambda, subln_g, sconv_w, w_out, ln_g, ln_b]
    slots = 2
    return pl.pallas_call(
        functools.partial(_attn_out_kernel, tq=tq, ck=ck),
        grid=(bsz, n // tq),
        in_specs=[tok(W), keys(nc), keys(n), keys(nc), keys(n), tok(W), tok(D_MODEL), tok(W), zprev, znext, tok(W),
                  _mod_spec(gate)] + [const(a) for a in consts],
        out_specs=tok(D_MODEL),
        out_shape=jax.ShapeDtypeStruct((bsz, n, D_MODEL), F32),
        scratch_shapes=[pltpu.VMEM((DIFF_HEADS, n + nc, DIFF_V_DIM), BF16),
                        pltpu.VMEM((DIFF_HEADS, n + nc, 2 * DIFF_V_DIM), BF16),
                        pltpu.VMEM((DIFF_HEADS, 2 * tq, DIFF_V_DIM), BF16),
                        pltpu.VMEM((DIFF_HEADS, tq, DIFF_V_DIM), F32),
                        pltpu.VMEM((slots, 2 * tq, max(ck)), F32), pltpu.VMEM((slots, 2 * tq, max(ck)), BF16),
                        pltpu.VMEM((slots, 2 * tq, LANES), F32), pltpu.VMEM((2 * tq, LANES), F32),
                        pltpu.VMEM((2 * tq, 2 * DIFF_V_DIM), F32)],
        compiler_params=_params("parallel", "arbitrary"),
        name="attention_out",
    )(q, kc, k, vc, v, ag, x, z, z, z, cg, gate, *consts)


def _pair_gate_weights(w_a, w_x):
    def pair_blockdiag(w):
        wp = w.reshape(N_PAIR, 2, LRU_HEAD_DIM, LRU_HEAD_DIM)
        zero = jnp.zeros_like(wp[:, 0])
        top = jnp.concatenate([wp[:, 0], zero], axis=-1)
        bot = jnp.concatenate([zero, wp[:, 1]], axis=-1)
        return jnp.concatenate([top, bot], axis=-2)
    return jnp.concatenate([pair_blockdiag(w_a[0]), pair_blockdiag(w_x[0]),
                            pair_blockdiag(w_a[1]), pair_blockdiag(w_x[1])], axis=-1).astype(BF16)


def _rope_tables(n):
    rows = n // GRID_W
    row = jnp.repeat(jnp.arange(rows, dtype=F32), GRID_W)
    col = jnp.tile(jnp.arange(GRID_W, dtype=F32), rows)
    n_freq = DIFF_HEAD_DIM // 4
    inv = ROPE_BASE ** (-jnp.arange(n_freq, dtype=F32) / n_freq)
    ang_r = row[:, None] * inv
    ang_c = col[:, None] * inv
    ang = jnp.concatenate([ang_r, ang_r, ang_c, ang_c], axis=-1)
    sign = jnp.tile(jnp.concatenate([-jnp.ones((n_freq,), F32), jnp.ones((n_freq,), F32)]), 2)
    reps = W // DIFF_HEAD_DIM
    return jnp.tile(jnp.cos(ang), (1, reps)), jnp.tile(jnp.sin(ang) * sign, (1, reps))


def kernel(x, c, ctx, c_ctx, w_mod, b_mod, ln_g, ln_b, ab_w_in, ab_w_out, lru_conv_w, lru_conv_b, lru_w_a, lru_b_a,
           lru_w_x, lru_b_x, lru_lambda, sgu_ln_g, sgu_ln_b, sgu_w, sgu_b, cd_w_in, cd_w_out, sconv_w, diff_lambda,
           diff_subln_g):
    bsz, n, _ = x.shape
    nc = ctx.shape[1]
    tiles = _tiling(n, nc)
    row = lambda a: a.reshape(1, -1)

    n_cond = -(-(bsz + 1) // SUBLANES) * SUBLANES
    cond = jnp.zeros((n_cond, D_MODEL), F32).at[:bsz].set(c).at[bsz].set(c_ctx)
    mod = _modulation(cond, w_mod, b_mod)

    def mods(l):
        lat = [mod[l, :bsz, j * D_MODEL:(j + 1) * D_MODEL][:, None, :] for j in range(3)]
        con = [mod[l, bsz:bsz + 1, j * D_MODEL:(j + 1) * D_MODEL][:, None, :] for j in range(3)]
        return lat, con

    (shift, scale, gate), (shift_c, scale_c, gate_c) = mods(0)
    w_in = ab_w_in[0].astype(BF16)
    w_out = ab_w_out[0].astype(BF16)
    sgu_args = (row(sgu_ln_g[0]), row(sgu_ln_b[0]), sgu_w[0].astype(BF16),
                jnp.broadcast_to(sgu_b[0][:, :, None], (SGU_GROUPS, SGU_CHUNK, LANES)))
    lx, ga, yb = _ab_in(x, shift, scale, w_in, *sgu_args, tm=tiles.proj_rows)
    lxc, gac, ybc = _ab_in(ctx, shift_c, scale_c, w_in, *sgu_args, tm=tiles.ctx_rows)
    wpair = _pair_gate_weights(lru_w_a[0], lru_w_x[0])
    ya, yac = _lru(lx, lxc, ga, gac, lru_conv_w[0], row(lru_conv_b[0]), wpair, lru_b_a[0], lru_b_x[0], lru_lambda[0])

    (shift, scale, gate1), (shift_c, scale_c, _) = mods(1)
    w_in = cd_w_in[0].astype(BF16)
    cos, sin_signed = _rope_tables(n)
    x1, z, cg, q, k, v, ag = _ab_out_cd_in(x, ya, yb, gate, w_out, row(ln_g[0]), row(ln_b[0]), shift, scale, w_in,
                                           cos, sin_signed, tm=tiles.proj_rows)
    kc, vc = _ctx_out_kv(ctx, yac, ybc, gate_c, w_out, row(ln_g[0]), row(ln_b[0]), shift_c, scale_c,
                         w_in[:, 5 * W:7 * W], tm=tiles.ctx_rows)
    gate = gate1
    w_out = cd_w_out[0].astype(BF16)
    return _attention_out(q, kc, k, vc, v, ag, x1, z, cg, gate, diff_lambda[0], row(diff_subln_g[0]), sconv_w[0], w_out,
                          row(ln_g[1]), row(ln_b[1]), tq=tiles.query_rows, ck=tiles.key_chunks)
```
